```python
import jax, jax.numpy as jnp
from jax import lax
import numpy as np

D_MODEL = 1024
BATCH = 4
SEQ = 4096
DEPTH = 1

GRID_W = 64
CTX_LEN = 256
HEAD_DIM = 64
NA_HEADS = 8
NA_WIN_R = 8
NA_WIN_C = 16
GQA_HEADS = 8
GQA_KV_HEADS = 2
Q_BLOCK = 128
ROPE_THETA = 10000.0
N_EXPERTS = 32
TOP_K = 4
D_FF = D_MODEL
SWIGLU_LIMIT = 7.0
SWIGLU_ALPHA = 1.702
EXPERT_BLOCK = 128
NORM_EPS = 1e-6

NA_WIDTH = NA_HEADS * HEAD_DIM
GQA_WIDTH = GQA_HEADS * HEAD_DIM
GQA_KV_WIDTH = GQA_KV_HEADS * HEAD_DIM
IN_SPLITS = (NA_WIDTH, NA_WIDTH, GQA_KV_WIDTH, GQA_KV_WIDTH, NA_WIDTH, GQA_WIDTH, D_MODEL, D_MODEL)
KV_COLS = 2 * NA_WIDTH + 2 * GQA_KV_WIDTH
IN_COLS = sum(IN_SPLITS)

kernel_name = 'hybrid_natten_gqa_moe_dit_layer'


def rmsnorm(x, g):
    xf = x.astype(jnp.float32)
    y = xf * lax.rsqrt(jnp.mean(xf * xf, axis=-1, keepdims=True) + NORM_EPS)
    return (y * g.astype(jnp.float32)).astype(x.dtype)


def split_cols(p, sizes):
    return jnp.split(p, np.cumsum(sizes)[:-1].tolist(), axis=-1)


def to_heads(t, n):
    return t.reshape(t.shape[0], t.shape[1], n, HEAD_DIM)


def axial_rope(x, row, col):
    half = HEAD_DIM // 2
    nf = half // 2
    freqs = ROPE_THETA ** (-jnp.arange(nf, dtype=jnp.float32) / nf)

    def rot(xp, pos):
        ang = pos.astype(jnp.float32)[:, None] * freqs
        cos = jnp.cos(ang)[None, :, None, :]
        sin = jnp.sin(ang)[None, :, None, :]
        x1 = xp[..., :nf].astype(jnp.float32)
        x2 = xp[..., nf:].astype(jnp.float32)
        return jnp.concatenate([x1 * cos - x2 * sin, x2 * cos + x1 * sin], axis=-1)

    return jnp.concatenate([rot(x[..., :half], row), rot(x[..., half:], col)], axis=-1).astype(x.dtype)


def gqa_attend(q, k, v):
    s = jnp.einsum('btkgd,bskd->bkgts', q, k)
    p = jax.nn.softmax(s.astype(jnp.float32), axis=-1).astype(q.dtype)
    return jnp.einsum('bkgts,bskd->btkgd', p, v)


def blocked_gqa(q, k, v):
    B, S = q.shape[0], q.shape[1]
    nb = S // Q_BLOCK
    qb = q.reshape(B, nb, Q_BLOCK, *q.shape[2:]).swapaxes(0, 1)
    out = lax.map(lambda blk: gqa_attend(blk, k, v), qb)
    return out.swapaxes(0, 1).reshape(B, S, -1)


def neighbourhood_attention(q, k, v, k_ctx, v_ctx, rpb):
    B, S, H, Dh = q.shape
    rows = S // GRID_W
    kr = min(NA_WIN_R, rows)
    kc = min(NA_WIN_C, GRID_W)
    qg = q.reshape(B, rows, GRID_W, H, Dh)
    kg = k.reshape(B, rows, GRID_W, H, Dh)
    vg = v.reshape(B, rows, GRID_W, H, Dh)
    cols = np.arange(GRID_W)
    c0 = np.clip(cols - kc // 2, 0, GRID_W - kc)
    col_idx = (c0[:, None] + np.arange(kc)[None, :]).astype(np.int32)
    col_off = (col_idx - cols[:, None] + (NA_WIN_C - 1)).astype(np.int32)

    def row_block(r):
        r0 = jnp.clip(r - kr // 2, 0, rows - kr)
        q_r = lax.dynamic_index_in_dim(qg, r, axis=1, keepdims=False)
        k_nb = lax.dynamic_slice_in_dim(kg, r0, kr, axis=1)[:, :, col_idx]
        v_nb = lax.dynamic_slice_in_dim(vg, r0, kr, axis=1)[:, :, col_idx]
        row_off = r0 + jnp.arange(kr, dtype=jnp.int32) - r + (NA_WIN_R - 1)
        bias = rpb[:, row_off[:, None, None], col_off[None, :, :]].transpose(0, 2, 1, 3)
        s_loc = jnp.einsum('bwhd,brwchd->bhwrc', q_r, k_nb) + bias.astype(q.dtype)
        s_ctx = jnp.einsum('bwhd,bnhd->bhwn', q_r, k_ctx)
        s = jnp.concatenate([s_loc.reshape(B, H, GRID_W, kr * kc), s_ctx], axis=-1)
        p = jax.nn.softmax(s.astype(jnp.float32), axis=-1).astype(q.dtype)
        p_loc = p[..., :kr * kc].reshape(B, H, GRID_W, kr, kc)
        p_ctx = p[..., kr * kc:]
        return (jnp.einsum('bhwrc,brwchd->bwhd', p_loc, v_nb)
                + jnp.einsum('bhwn,bnhd->bwhd', p_ctx, v_ctx))

    out = lax.map(row_block, jnp.arange(rows, dtype=jnp.int32))
    return out.transpose(1, 0, 2, 3, 4).reshape(B, S, H * Dh)


def merge_branches(o_a, o_b, g_a, g_b, w_out_a, w_out_b, w_o):
    return (jax.nn.sigmoid(g_a) * (o_a @ w_out_a) + jax.nn.sigmoid(g_b) * (o_b @ w_out_b)) @ w_o


def hybrid_mixer(h, h_ctx, w_in, rpb, g_qn, g_kn, w_out_a, w_out_b, w_o, row, col, update_ctx):
    B, S, _ = h.shape
    C = h_ctx.shape[1]
    scale = HEAD_DIM ** -0.5
    G = GQA_HEADS // GQA_KV_HEADS
    k_a, v_a, k_b, v_b, q_a, q_b, g_a, g_b = split_cols(h @ w_in, IN_SPLITS)
    if update_ctx:
        parts = split_cols(h_ctx @ w_in, IN_SPLITS)
    else:
        parts = split_cols(h_ctx @ w_in[:, :KV_COLS], IN_SPLITS[:4])
    k_ca = to_heads(parts[0], NA_HEADS)
    v_ca = to_heads(parts[1], NA_HEADS)
    k_cb = rmsnorm(to_heads(parts[2], GQA_KV_HEADS), g_kn)
    v_cb = to_heads(parts[3], GQA_KV_HEADS)

    o_a = neighbourhood_attention(to_heads(q_a, NA_HEADS) * scale, to_heads(k_a, NA_HEADS),
                                  to_heads(v_a, NA_HEADS), k_ca, v_ca, rpb)
    qb = axial_rope(rmsnorm(to_heads(q_b, GQA_HEADS), g_qn), row, col) * scale
    kb = axial_rope(rmsnorm(to_heads(k_b, GQA_KV_HEADS), g_kn), row, col)
    k_all = jnp.concatenate([kb, k_cb], axis=1)
    v_all = jnp.concatenate([to_heads(v_b, GQA_KV_HEADS), v_cb], axis=1)
    o_b = blocked_gqa(qb.reshape(B, S, GQA_KV_HEADS, G, HEAD_DIM), k_all, v_all)
    y = merge_branches(o_a, o_b, g_a, g_b, w_out_a, w_out_b, w_o)

    y_ctx = None
    if update_ctx:
        q_ca, q_cb, g_ca, g_cb = parts[4:]
        o_ca = gqa_attend(to_heads(q_ca, NA_HEADS)[:, :, :, None, :] * scale, k_ca, v_ca).reshape(B, C, -1)
        qcb = rmsnorm(to_heads(q_cb, GQA_HEADS), g_qn) * scale
        o_cb = gqa_attend(qcb.reshape(B, C, GQA_KV_HEADS, G, HEAD_DIM), k_cb, v_cb).reshape(B, C, -1)
        y_ctx = merge_branches(o_ca, o_cb, g_ca, g_cb, w_out_a, w_out_b, w_o)
    return y, y_ctx


def moe_ffn(h, w_router, b_router, w_gu, b_gu, w_dn, b_dn):
    T, D = h.shape
    logits = (h @ w_router + b_router).astype(jnp.float32)
    top_logits, top_e = lax.top_k(logits, TOP_K)
    top_w = jax.nn.softmax(top_logits, axis=-1).astype(h.dtype)
    n_assign = T * TOP_K
    flat_e = top_e.reshape(-1)
    flat_tok = jnp.arange(n_assign, dtype=jnp.int32) // TOP_K
    order = jnp.argsort(flat_e)
    e_sorted = flat_e[order]
    counts = jnp.bincount(flat_e, length=N_EXPERTS)
    padded = (counts + EXPERT_BLOCK - 1) // EXPERT_BLOCK * EXPERT_BLOCK
    pad_end = jnp.cumsum(padded)
    pad_start = pad_end - padded
    start = jnp.cumsum(counts) - counts
    dest = pad_start[e_sorted] + jnp.arange(n_assign, dtype=jnp.int32) - start[e_sorted]
    n_blocks = -(-n_assign // EXPERT_BLOCK) + N_EXPERTS
    n_slots = n_blocks * EXPERT_BLOCK
    slot_tok = jnp.full((n_slots,), T, dtype=jnp.int32).at[dest].set(flat_tok[order])
    slot_w = jnp.zeros((n_slots,), h.dtype).at[dest].set(top_w.reshape(-1)[order])
    block_e = jnp.minimum(jnp.searchsorted(pad_end, jnp.arange(n_blocks, dtype=jnp.int32) * EXPERT_BLOCK,
                                           side='right'), N_EXPERTS - 1)
    h_pad = jnp.concatenate([h, jnp.zeros((1, D), h.dtype)], axis=0)
    xb = h_pad[slot_tok].reshape(n_blocks, EXPERT_BLOCK, D)

    def expert_block(args):
        xe, e = args
        gu = xe @ w_gu[e] + b_gu[e]
        x_glu = jnp.minimum(gu[:, :D_FF], SWIGLU_LIMIT)
        x_lin = jnp.clip(gu[:, D_FF:], -SWIGLU_LIMIT, SWIGLU_LIMIT)
        act = (x_lin + 1) * (x_glu * jax.nn.sigmoid(SWIGLU_ALPHA * x_glu))
        return act @ w_dn[e] + b_dn[e]

    yb = lax.map(expert_block, (xb, block_e))
    y = jnp.zeros((T + 1, D), h.dtype).at[slot_tok].add(yb.reshape(n_slots, D) * slot_w[:, None])
    return y[:T]


def setup_inputs(seed: int = 0) -> dict:
    key = jax.random.key(seed)
    ks = jax.random.split(key, 23)
    f32 = jnp.float32
    L = DEPTH

    def nrm(k, shape, s):
        return jax.random.normal(k, shape, f32) * s

    def gain(k, shape):
        return 1.0 + 0.05 * jax.random.normal(k, shape, f32)

    return {
        'x': nrm(ks[0], (BATCH, SEQ, D_MODEL), 1.0),
        'c': nrm(ks[1], (BATCH, D_MODEL), 1.0),
        'ctx': nrm(ks[2], (BATCH, CTX_LEN, D_MODEL), 1.0),
        'c_ctx': nrm(ks[3], (D_MODEL,), 1.0),
        'w_mod': nrm(ks[4], (L, D_MODEL, 6 * D_MODEL), 0.5 * D_MODEL ** -0.5),
        'b_mod': nrm(ks[5], (L, 6 * D_MODEL), 0.02),
        'g_pre_mix': gain(ks[6], (L, D_MODEL)),
        'g_post_mix': gain(ks[7], (L, D_MODEL)),
        'g_pre_ffn': gain(ks[8], (L, D_MODEL)),
        'g_post_ffn': gain(ks[9], (L, D_MODEL)),
        'w_in': nrm(ks[10], (L, D_MODEL, IN_COLS), D_MODEL ** -0.5),
        'rpb': nrm(ks[11], (L, NA_HEADS, 2 * NA_WIN_R - 1, 2 * NA_WIN_C - 1), 0.1),
        'g_qnorm': gain(ks[12], (L, HEAD_DIM)),
        'g_knorm': gain(ks[13], (L, HEAD_DIM)),
        'w_out_a': nrm(ks[14], (L, NA_WIDTH, D_MODEL), NA_WIDTH ** -0.5),
        'w_out_b': nrm(ks[15], (L, GQA_WIDTH, D_MODEL), GQA_WIDTH ** -0.5),
        'w_o': nrm(ks[16], (L, D_MODEL, D_MODEL), D_MODEL ** -0.5),
        'w_router': nrm(ks[17], (L, D_MODEL, N_EXPERTS), D_MODEL ** -0.5),
        'b_router': nrm(ks[18], (L, N_EXPERTS), 0.01),
        'w_gu': nrm(ks[19], (L, N_EXPERTS, D_MODEL, 2 * D_FF), D_MODEL ** -0.5),
        'b_gu': nrm(ks[20], (L, N_EXPERTS, 2 * D_FF), 0.02),
        'w_dn': nrm(ks[21], (L, N_EXPERTS, D_FF, D_MODEL), D_FF ** -0.5),
        'b_dn': nrm(ks[22], (L, N_EXPERTS, D_MODEL), 0.02),
    }


def reference(x, c, ctx, c_ctx, w_mod, b_mod, g_pre_mix, g_post_mix, g_pre_ffn, g_post_ffn,
              w_in, rpb, g_qnorm, g_knorm, w_out_a, w_out_b, w_o,
              w_router, b_router, w_gu, b_gu, w_dn, b_dn):
    B, S, D = x.shape
    C = ctx.shape[1]
    t = jnp.arange(S, dtype=jnp.int32)
    row = t // GRID_W
    col = t % GRID_W
    silu_c = jax.nn.silu(c)
    silu_cc = jax.nn.silu(c_ctx)
    for i in range(DEPTH):
        update_ctx = i < DEPTH - 1
        sh1, sc1, gt1, sh2, sc2, gt2 = jnp.split((silu_c @ w_mod[i] + b_mod[i])[:, None, :], 6, axis=-1)
        n_mod = 6 if update_ctx else 2
        mod_c = jnp.split(silu_cc @ w_mod[i][:, :n_mod * D] + b_mod[i][:n_mod * D], n_mod)

        h = rmsnorm(x, g_pre_mix[i]) * (1 + sc1) + sh1
        h_ctx = rmsnorm(ctx, g_pre_mix[i]) * (1 + mod_c[1]) + mod_c[0]
        y, y_ctx = hybrid_mixer(h, h_ctx, w_in[i], rpb[i], g_qnorm[i], g_knorm[i],
                                w_out_a[i], w_out_b[i], w_o[i], row, col, update_ctx)
        x = x + gt1 * rmsnorm(y, g_post_mix[i])

        h2 = rmsnorm(x, g_pre_ffn[i]) * (1 + sc2) + sh2
        tokens = h2.reshape(B * S, D)
        if update_ctx:
            ctx = ctx + mod_c[2] * rmsnorm(y_ctx, g_post_mix[i])
            h2_ctx = rmsnorm(ctx, g_pre_ffn[i]) * (1 + mod_c[4]) + mod_c[3]
            tokens = jnp.concatenate([tokens, h2_ctx.reshape(B * C, D)], axis=0)
        f = moe_ffn(tokens, w_router[i], b_router[i], w_gu[i], b_gu[i], w_dn[i], b_dn[i])
        x = x + gt2 * rmsnorm(f[:B * S].reshape(B, S, D), g_post_ffn[i])
        if update_ctx:
            ctx = ctx + mod_c[5] * rmsnorm(f[B * S:].reshape(B, C, D), g_post_ffn[i])
    return x
```

```python
import functools

import numpy as np
import jax
import jax.numpy as jnp
from jax import lax
from jax.experimental import pallas as pl
from jax.experimental.pallas import tpu as pltpu

F32 = jnp.float32
BF16 = jnp.bfloat16

D_MODEL = 1024
GRID_W = 64
HEAD_DIM = 64
NA_HEADS = 8
NA_WIN_R = 8
NA_WIN_C = 16
GQA_HEADS = 8
GQA_KV_HEADS = 2
ROPE_THETA = 10000.0
N_EXPERTS = 32
TOP_K = 4
D_FF = D_MODEL
SWIGLU_LIMIT = 7.0
SWIGLU_ALPHA = 1.702
EXPERT_BLOCK = 128
NORM_EPS = 1e-6

NA_WIDTH = NA_HEADS * HEAD_DIM
GQA_WIDTH = GQA_HEADS * HEAD_DIM
GQA_KV_WIDTH = GQA_KV_HEADS * HEAD_DIM
KV_COLS = 2 * NA_WIDTH + 2 * GQA_KV_WIDTH
QKV_COLS = KV_COLS + NA_WIDTH + GQA_WIDTH
GATE_COLS = 2 * D_MODEL

LANES = 128
NEG_BIG = -1e30
VMEM_LIMIT = 56 * 1024 * 1024

NA_Q_ROWS = 2
NA_K_ROWS = 10
GQA_TQ = 128
GQA_KC = 256


def _cparams(*sem):
    return pltpu.CompilerParams(dimension_semantics=sem, vmem_limit_bytes=VMEM_LIMIT)


def _split_bf16(a):
    hi = a.astype(BF16)
    lo = (a - hi.astype(F32)).astype(BF16)
    return hi, lo


def _dot(a, b):
    return jnp.dot(a, b, preferred_element_type=F32)


def _dot_nt(a, b):
    return lax.dot_general(a, b, (((1,), (1,)), ((), ())), preferred_element_type=F32)


def _rms(xf):
    return xf * lax.rsqrt(jnp.mean(xf * xf, axis=-1, keepdims=True) + NORM_EPS)


def _mod_kernel(c_ref, whi_ref, wlo_ref, b_ref, o_ref):
    c = c_ref[...]
    s = c * jax.nn.sigmoid(c)
    shi, slo = _split_bf16(s)
    whi = whi_ref[...]
    o_ref[...] = _dot(shi, whi) + _dot(slo, whi) + _dot(shi, wlo_ref[...]) + b_ref[...]


def _modulation(cin, w_mod, b_mod):
    n = w_mod.shape[1]
    bn = 1024
    whi, wlo = _split_bf16(w_mod)
    return pl.pallas_call(
        _mod_kernel,
        grid=(n // bn,),
        in_specs=[pl.BlockSpec((8, D_MODEL), lambda j: (0, 0)),
                  pl.BlockSpec((D_MODEL, bn), lambda j: (0, j)),
                  pl.BlockSpec((D_MODEL, bn), lambda j: (0, j)),
                  pl.BlockSpec((1, bn), lambda j: (0, j))],
        out_specs=pl.BlockSpec((8, bn), lambda j: (0, j)),
        out_shape=jax.ShapeDtypeStruct((8, n), F32),
        compiler_params=_cparams("arbitrary"),
        name="mod",
    )(cin, whi, wlo, b_mod.reshape(1, n))


def _head_norm(y, ones_bd, g):
    hi, lo = _split_bf16(y * y)
    ms = (_dot(hi, ones_bd) + _dot(lo, ones_bd)) * (1.0 / HEAD_DIM)
    return y * lax.rsqrt(ms + NORM_EPS) * g


def _rope(y, cos, sin):
    w = y.shape[1]
    lane = lax.broadcasted_iota(jnp.int32, y.shape, 1)
    up = pltpu.roll(y, w - 16, 1)
    dn = pltpu.roll(y, 16, 1)
    sw = jnp.where((lane % 32) < 16, up, dn)
    return y * cos + sw * sin


def _inproj_kernel(x_ref, mod_ref, g_ref, w_ref, cos_ref, sin_ref, gq_ref, gk_ref, ones_ref,
                   ka_ref, va_ref, kb_ref, vb_ref, *q_refs, rope, with_q):
    scale = HEAD_DIM ** -0.5
    if with_q:
        qa_ref, qb_ref = q_refs
    xf = x_ref[...]
    h = _rms(xf) * g_ref[...] * (1.0 + mod_ref[1:2, :]) + mod_ref[0:1, :]
    y = _dot(h.astype(BF16), w_ref[...])
    ka_ref[...] = y[:, 0:NA_WIDTH].astype(BF16)
    va_ref[...] = y[:, NA_WIDTH:2 * NA_WIDTH].astype(BF16)
    o = 2 * NA_WIDTH
    kb = _head_norm(y[:, o:o + GQA_KV_WIDTH], ones_ref[0:GQA_KV_WIDTH, 0:GQA_KV_WIDTH], gk_ref[...])
    if rope:
        cos = cos_ref[...]
        sin = sin_ref[...]
        kb = _rope(kb, cos, sin)
    kb_ref[...] = kb.astype(BF16)
    vb_ref[...] = y[:, o + GQA_KV_WIDTH:KV_COLS].astype(BF16)
    if with_q:
        qa_ref[...] = (y[:, KV_COLS:KV_COLS + NA_WIDTH] * scale).astype(BF16)
        qb = _head_norm(y[:, KV_COLS + NA_WIDTH:QKV_COLS], ones_ref[...], gq_ref[...])
        if rope:
            reps = GQA_WIDTH // LANES
            qb = _rope(qb, jnp.concatenate([cos] * reps, axis=1), jnp.concatenate([sin] * reps, axis=1))
        qb_ref[...] = (qb * scale).astype(BF16)


def _inproj(x, mod, g_pre, w, cos, sin, gq, gk, ones_bd, *, tm, rope, with_q):
    b, s, d = x.shape
    ncols = w.shape[1]
    tok = lambda i, j: (i, j, 0)
    const = lambda i, j: (0, 0)
    out_w = [NA_WIDTH, NA_WIDTH, GQA_KV_WIDTH, GQA_KV_WIDTH] + ([NA_WIDTH, GQA_WIDTH] if with_q else [])
    kern = functools.partial(_inproj_kernel, rope=rope, with_q=with_q)
    return pl.pallas_call(
        kern,
        grid=(b, s // tm),
        in_specs=[pl.BlockSpec((None, tm, d), tok),
                  pl.BlockSpec((None, 6, d), lambda i, j: (i, 0, 0)),
                  pl.BlockSpec((1, d), const),
                  pl.BlockSpec((d, ncols), const),
                  pl.BlockSpec((tm, LANES), lambda i, j: (j, 0)),
                  pl.BlockSpec((tm, LANES), lambda i, j: (j, 0)),
                  pl.BlockSpec((1, GQA_WIDTH), const),
                  pl.BlockSpec((1, GQA_KV_WIDTH), const),
                  pl.BlockSpec((GQA_WIDTH, GQA_WIDTH), const)],
        out_specs=[pl.BlockSpec((None, tm, wd), tok) for wd in out_w],
        out_shape=[jax.ShapeDtypeStruct((b, s, wd), BF16) for wd in out_w],
        compiler_params=_cparams("arbitrary", "arbitrary"),
        name="inproj_rope" if rope else "inproj_ctx",
    )(x, mod, g_pre, w, cos, sin, gq, gk, ones_bd)


def _na_base(i):
    return jnp.clip(i * NA_Q_ROWS - NA_WIN_R // 2, 0, GRID_W - NA_K_ROWS)


def _na_bias_tables(rpb):
    rows = GRID_W
    n_blk = rows // NA_Q_ROWS
    qr = np.arange(n_blk) * NA_Q_ROWS
    base = np.clip(qr - NA_WIN_R // 2, 0, rows - NA_K_ROWS)
    offs = qr - base
    n_off = int(offs.max()) + 1
    rep_qr = np.zeros(n_off, np.int64)
    for o, q in zip(offs, qr):
        rep_qr[o] = q
    rep_base = np.clip(rep_qr - NA_WIN_R // 2, 0, rows - NA_K_ROWS)
    r = rep_qr[:, None] + np.arange(NA_Q_ROWS)[None, :]
    r0 = np.clip(r - NA_WIN_R // 2, 0, rows - NA_WIN_R)
    kr = rep_base[:, None] + np.arange(NA_K_ROWS)[None, :]
    row_ok = (kr[:, None, :] >= r0[:, :, None]) & (kr[:, None, :] < r0[:, :, None] + NA_WIN_R)
    row_off = np.clip(kr[:, None, :] - r[:, :, None] + NA_WIN_R - 1, 0, 2 * NA_WIN_R - 2)
    c = np.arange(GRID_W)
    c0 = np.clip(c - NA_WIN_C // 2, 0, GRID_W - NA_WIN_C)
    col_ok = (c[None, :] >= c0[:, None]) & (c[None, :] < c0[:, None] + NA_WIN_C)
    col_off = np.clip(c[None, :] - c[:, None] + NA_WIN_C - 1, 0, 2 * NA_WIN_C - 2)
    ok = row_ok[:, :, None, :, None] & col_ok[None, None, :, None, :]
    ro = np.broadcast_to(row_off[:, :, None, :, None], ok.shape)
    co = np.broadcast_to(col_off[None, None, :, None, :], ok.shape)
    vals = rpb[:, ro, co]
    tab = jnp.where(ok[None], vals, NEG_BIG)
    tab = tab.reshape(NA_HEADS, n_off, NA_Q_ROWS * GRID_W, NA_K_ROWS * GRID_W)
    return jnp.transpose(tab, (1, 0, 2, 3)).astype(F32)


def _natten_kernel(q_ref, k_ref, v_ref, kc_ref, vc_ref, bias_ref, o_ref):
    i = pl.program_id(1)
    start = pl.multiple_of(_na_base(i) * GRID_W, GRID_W)
    nk = NA_K_ROWS * GRID_W
    slab = 4 * HEAD_DIM
    head_of_lane = lax.broadcasted_iota(jnp.int32, (1, slab), 1) // HEAD_DIM
    for sl in range(NA_HEADS // 4):
        cols = slice(sl * slab, (sl + 1) * slab)
        q4 = q_ref[:, cols]
        k4 = k_ref[pl.ds(start, nk), cols]
        v4 = v_ref[pl.ds(start, nk), cols]
        kc4 = kc_ref[:, cols]
        vc4 = vc_ref[:, cols]
        acc = jnp.zeros((q4.shape[0], slab), F32)
        for g in range(4):
            sel = head_of_lane == g
            qm = jnp.where(sel, q4, jnp.zeros_like(q4))
            s_loc = _dot_nt(qm, k4) + bias_ref[sl * 4 + g]
            s_ctx = _dot_nt(qm, kc4)
            m = jnp.maximum(jnp.max(s_loc, axis=-1, keepdims=True), jnp.max(s_ctx, axis=-1, keepdims=True))
            p_loc = jnp.exp(s_loc - m)
            p_ctx = jnp.exp(s_ctx - m)
            l = jnp.sum(p_loc, axis=-1, keepdims=True) + jnp.sum(p_ctx, axis=-1, keepdims=True)
            o = _dot(p_loc.astype(BF16), v4) + _dot(p_ctx.astype(BF16), vc4)
            acc = acc + jnp.where(sel, o * (1.0 / l), 0.0)
        o_ref[:, cols] = acc.astype(BF16)


def _natten(q_a, k_a, v_a, k_ca, v_ca, bias):
    b, s, w = q_a.shape
    c = k_ca.shape[1]
    tq = NA_Q_ROWS * GRID_W
    nk = NA_K_ROWS * GRID_W
    full = lambda i, j: (i, 0, 0)
    return pl.pallas_call(
        _natten_kernel,
        grid=(b, s // tq),
        in_specs=[pl.BlockSpec((None, tq, w), lambda i, j: (i, j, 0)),
                  pl.BlockSpec((None, s, w), full),
                  pl.BlockSpec((None, s, w), full),
                  pl.BlockSpec((None, c, w), full),
                  pl.BlockSpec((None, c, w), full),
                  pl.BlockSpec((None, NA_HEADS, tq, nk),
                               lambda i, j: (j * NA_Q_ROWS - _na_base(j), 0, 0, 0))],
        out_specs=pl.BlockSpec((None, tq, w), lambda i, j: (i, j, 0)),
        out_shape=jax.ShapeDtypeStruct((b, s, w), BF16),
        compiler_params=_cparams("arbitrary", "arbitrary"),
        name="natten",
    )(q_a, k_a, v_a, k_ca, v_ca, bias)


def _gqa_kernel(q_ref, kt_ref, v_ref, o_ref):
    group = GQA_HEADS // GQA_KV_HEADS
    n_chunks = kt_ref.shape[1]
    tq = q_ref.shape[0]
    m_rows = group * tq
    for j in range(GQA_KV_HEADS):
        qs = jnp.concatenate(
            [q_ref[:, (j * group + g) * HEAD_DIM:(j * group + g + 1) * HEAD_DIM] for g in range(group)], axis=0)

        def body(c, carry, j=j, qs=qs):
            m, l, acc = carry
            s = _dot(qs, kt_ref[j, c])
            m_new = jnp.maximum(m, jnp.max(s, axis=-1, keepdims=True))
            alpha = jnp.exp(m - m_new)
            p = jnp.exp(s - m_new)
            l = alpha * l + jnp.sum(p, axis=-1, keepdims=True)
            acc = alpha * acc + _dot(p.astype(BF16), v_ref[c])
            return m_new, l, acc

        init = (jnp.full((m_rows, 1), NEG_BIG, F32), jnp.zeros((m_rows, 1), F32),
                jnp.zeros((m_rows, GQA_KV_WIDTH), F32))
        _, l, acc = lax.fori_loop(0, n_chunks, body, init)
        o = acc[:, j * HEAD_DIM:(j + 1) * HEAD_DIM] * (1.0 / l)
        for g in range(group):
            h = j * group + g
            o_ref[:, h * HEAD_DIM:(h + 1) * HEAD_DIM] = o[g * tq:(g + 1) * tq].astype(BF16)


def _gqa(q_b, kt, v):
    b, s, w = q_b.shape
    _, nkv, nch, hd, kc = kt.shape
    return pl.pallas_call(
        _gqa_kernel,
        grid=(b, s // GQA_TQ),
        in_specs=[pl.BlockSpec((None, GQA_TQ, w), lambda i, j: (i, j, 0)),
                  pl.BlockSpec((None, nkv, nch, hd, kc), lambda i, j: (i, 0, 0, 0, 0)),
                  pl.BlockSpec((None, nch, kc, GQA_KV_WIDTH), lambda i, j: (i, 0, 0, 0))],
        out_specs=pl.BlockSpec((None, GQA_TQ, w), lambda i, j: (i, j, 0)),
        out_shape=jax.ShapeDtypeStruct((b, s, w), BF16),
        compiler_params=_cparams("arbitrary", "arbitrary"),
        name="gqa",
    )(q_b, kt, v)


def _merge_kernel(x_ref, oa_ref, ob_ref, mod_ref, g1_ref, g2_ref, g3_ref, wg_ref, woa_ref, wob_ref, wo_ref,
                  wrh_ref, wrl_ref, br_ref, x1_ref, h2_ref, route_ref):
    xf = x_ref[...]
    h = _rms(xf) * g1_ref[...] * (1.0 + mod_ref[1:2, :]) + mod_ref[0:1, :]
    gates = _dot(h.astype(BF16), wg_ref[...])
    ya = _dot(oa_ref[...], woa_ref[...])
    yb = _dot(ob_ref[...], wob_ref[...])
    z = jax.nn.sigmoid(gates[:, :D_MODEL]) * ya + jax.nn.sigmoid(gates[:, D_MODEL:]) * yb
    y = _dot(z.astype(BF16), wo_ref[...])
    x1 = xf + mod_ref[2:3, :] * (_rms(y) * g2_ref[...])
    x1_ref[...] = x1
    h2 = _rms(x1) * g3_ref[...] * (1.0 + mod_ref[4:5, :]) + mod_ref[3:4, :]

    half = D_MODEL // 2
    lo_bits = lax.bitcast_convert_type(h2[:, :half].astype(BF16).astype(F32), jnp.uint32) >> 16
    hi_bits = lax.bitcast_convert_type(h2[:, half:].astype(BF16).astype(F32), jnp.uint32) & jnp.uint32(0xFFFF0000)
    h2_ref[...] = lo_bits | hi_bits

    hh, hl = _split_bf16(h2)
    wrh = wrh_ref[...]
    lg = _dot(hh, wrh) + _dot(hl, wrh) + _dot(hh, wrl_ref[...]) + br_ref[...]
    lane = lax.broadcasted_iota(jnp.int32, lg.shape, 1)
    vals, idxs = [], []
    for _ in range(TOP_K):
        m = jnp.max(lg, axis=-1, keepdims=True)
        idx = jnp.min(jnp.where(lg == m, lane, LANES), axis=-1, keepdims=True)
        vals.append(m)
        idxs.append(idx)
        lg = jnp.where(lane == idx, -3e38, lg)
    es = [jnp.exp(v - vals[0]) for v in vals]
    inv = 1.0 / (es[0] + es[1] + es[2] + es[3])
    route = jnp.zeros(lg.shape, F32)
    for k in range(TOP_K):
        route = jnp.where(lane == k, idxs[k].astype(F32), route)
        route = jnp.where(lane == TOP_K + k, es[k] * inv, route)
    route_ref[...] = route


def _merge(x, o_a, o_b, mod, g1, g2, g3, wg, woa, wob, wo, wrh, wrl, br, *, tm):
    b, s, d = x.shape
    tok = lambda i, j: (i, j, 0)
    const = lambda i, j: (0, 0)
    return pl.pallas_call(
        _merge_kernel,
        grid=(b, s // tm),
        in_specs=[pl.BlockSpec((None, tm, d), tok),
                  pl.BlockSpec((None, tm, NA_WIDTH), tok),
                  pl.BlockSpec((None, tm, GQA_WIDTH), tok),
                  pl.BlockSpec((None, 6, d), lambda i, j: (i, 0, 0)),
                  pl.BlockSpec((1, d), const), pl.BlockSpec((1, d), const), pl.BlockSpec((1, d), const),
                  pl.BlockSpec((d, GATE_COLS), const),
                  pl.BlockSpec((NA_WIDTH, d), const),
                  pl.BlockSpec((GQA_WIDTH, d), const),
                  pl.BlockSpec((d, d), const),
                  pl.BlockSpec((d, LANES), const), pl.BlockSpec((d, LANES), const),
                  pl.BlockSpec((1, LANES), const)],
        out_specs=[pl.BlockSpec((None, tm, d), tok),
                   pl.BlockSpec((None, tm, d // 2), tok),
                   pl.BlockSpec((None, tm, LANES), tok)],
        out_shape=[jax.ShapeDtypeStruct((b, s, d), F32),
                   jax.ShapeDtypeStruct((b, s, d // 2), jnp.uint32),
                   jax.ShapeDtypeStruct((b, s, LANES), F32)],
        compiler_params=_cparams("arbitrary", "arbitrary"),
        name="merge",
    )(x, o_a, o_b, mod, g1, g2, g3, wg, woa, wob, wo, wrh, wrl, br)


def _row_copy(src_hbm, dst_vmem, sem, row, i):
    return pltpu.make_async_copy(src_hbm.at[pl.ds(row, 1)], dst_vmem.at[pl.ds(i, 1)], sem)


def _dispatch_kernel(tok_ref, nused_ref, h2_hbm, o_ref, sem, *, n_tok):
    blk = pl.program_id(0)
    rows = o_ref.shape[0]

    @pl.when(blk < nused_ref[0])
    def _():
        def issue(i, carry):
            t = tok_ref[0, i]

            @pl.when(t < n_tok)
            def _():
                _row_copy(h2_hbm, o_ref, sem, t, i).start()

            @pl.when(t >= n_tok)
            def _():
                o_ref[pl.ds(i, 1), :] = jnp.zeros((1, o_ref.shape[1]), o_ref.dtype)
            return carry

        lax.fori_loop(0, rows, issue, 0)

        def drain(i, carry):
            @pl.when(tok_ref[0, i] < n_tok)
            def _():
                _row_copy(h2_hbm, o_ref, sem, 0, i).wait()
            return carry

        lax.fori_loop(0, rows, drain, 0)

    @pl.when(blk >= nused_ref[0])
    def _():
        o_ref[...] = jnp.zeros(o_ref.shape, o_ref.dtype)


def _dispatch(slot_tok, nused, h2p, n_blocks):
    n_tok, wd = h2p.shape
    kern = functools.partial(_dispatch_kernel, n_tok=n_tok)
    return pl.pallas_call(
        kern,
        grid=(n_blocks,),
        in_specs=[pl.BlockSpec((None, 1, EXPERT_BLOCK), lambda i: (i, 0, 0), memory_space=pltpu.SMEM),
                  pl.BlockSpec(memory_space=pltpu.SMEM),
                  pl.BlockSpec(memory_space=pl.ANY)],
        out_specs=pl.BlockSpec((EXPERT_BLOCK, wd), lambda i: (i, 0)),
        out_shape=jax.ShapeDtypeStruct((n_blocks * EXPERT_BLOCK, wd), jnp.uint32),
        scratch_shapes=[pltpu.SemaphoreType.DMA(())],
        compiler_params=_cparams("arbitrary"),
        name="dispatch",
    )(slot_tok.reshape(n_blocks, 1, EXPERT_BLOCK), nused, h2p)


def _expert_kernel(be_ref, nused_ref, x_ref, wgu_ref, bgu_ref, wdn_ref, bdn_ref, o_ref):
    blk = pl.program_id(0)

    @pl.when(blk < nused_ref[0])
    def _():
        p = x_ref[...]
        xa = lax.bitcast_convert_type(p << 16, F32)
        xb = lax.bitcast_convert_type(p & jnp.uint32(0xFFFF0000), F32)
        x = jnp.concatenate([xa, xb], axis=1).astype(BF16)
        gu = _dot(x, wgu_ref[...]) + bgu_ref[...]
        x_glu = jnp.minimum(gu[:, :D_FF], SWIGLU_LIMIT)
        x_lin = jnp.clip(gu[:, D_FF:], -SWIGLU_LIMIT, SWIGLU_LIMIT)
        act = (x_lin + 1.0) * (x_glu * jax.nn.sigmoid(SWIGLU_ALPHA * x_glu))
        o_ref[...] = _dot(act.astype(BF16), wdn_ref[...]) + bdn_ref[...]

    @pl.when(blk >= nused_ref[0])
    def _():
        o_ref[...] = jnp.zeros(o_ref.shape, o_ref.dtype)


def _experts(block_e, nused, xb, w_gu, b_gu, w_dn, b_dn):
    n_slots, wd = xb.shape
    n_blocks = n_slots // EXPERT_BLOCK
    grid_spec = pltpu.PrefetchScalarGridSpec(
        num_scalar_prefetch=2,
        grid=(n_blocks,),
        in_specs=[pl.BlockSpec((EXPERT_BLOCK, wd), lambda i, be, nu: (i, 0)),
                  pl.BlockSpec((None, D_MODEL, 2 * D_FF), lambda i, be, nu: (be[i], 0, 0)),
                  pl.BlockSpec((None, 1, 2 * D_FF), lambda i, be, nu: (be[i], 0, 0)),
                  pl.BlockSpec((None, D_FF, D_MODEL), lambda i, be, nu: (be[i], 0, 0)),
                  pl.BlockSpec((None, 1, D_MODEL), lambda i, be, nu: (be[i], 0, 0))],
        out_specs=pl.BlockSpec((EXPERT_BLOCK, D_MODEL), lambda i, be, nu: (i, 0)),
    )
    return pl.pallas_call(
        _expert_kernel,
        grid_spec=grid_spec,
        out_shape=jax.ShapeDtypeStruct((n_slots, D_MODEL), F32),
        compiler_params=_cparams("arbitrary"),
        name="experts",
    )(block_e, nused, xb, w_gu, b_gu.reshape(N_EXPERTS, 1, 2 * D_FF), w_dn, b_dn.reshape(N_EXPERTS, 1, D_MODEL))


def _combine_kernel(dest_ref, yb_hbm, x1_ref, w_ref, mod_ref, g_ref, o_ref, buf, sem):
    tm = x1_ref.shape[0]

    def issue(i, carry):
        for k in range(TOP_K):
            pltpu.make_async_copy(yb_hbm.at[pl.ds(dest_ref[0, i * TOP_K + k], 1)],
                                  buf.at[k, pl.ds(i, 1)], sem).start()
        return carry

    lax.fori_loop(0, tm, issue, 0)

    def drain(i, carry):
        for k in range(TOP_K):
            pltpu.make_async_copy(yb_hbm.at[pl.ds(0, 1)], buf.at[k, pl.ds(i, 1)], sem).wait()
        return carry

    lax.fori_loop(0, tm, drain, 0)

    w = w_ref[...]
    f = buf[0] * w[:, TOP_K:TOP_K + 1]
    for k in range(1, TOP_K):
        f = f + buf[k] * w[:, TOP_K + k:TOP_K + k + 1]
    o_ref[...] = x1_ref[...] + mod_ref[5:6, :] * (_rms(f) * g_ref[...])


def _combine(dest, yb, x1, route, mod, g_post, *, tm):
    b, s, d = x1.shape
    nt = s // tm
    tok = lambda i, j: (i, j, 0)
    return pl.pallas_call(
        _combine_kernel,
        grid=(b, nt),
        in_specs=[pl.BlockSpec((None, 1, tm * TOP_K), lambda i, j: (i * nt + j, 0, 0), memory_space=pltpu.SMEM),
                  pl.BlockSpec(memory_space=pl.ANY),
                  pl.BlockSpec((None, tm, d), tok),
                  pl.BlockSpec((None, tm, LANES), tok),
                  pl.BlockSpec((None, 6, d), lambda i, j: (i, 0, 0)),
                  pl.BlockSpec((1, d), lambda i, j: (0, 0))],
        out_specs=pl.BlockSpec((None, tm, d), tok),
        out_shape=jax.ShapeDtypeStruct((b, s, d), F32),
        scratch_shapes=[pltpu.VMEM((TOP_K, tm, d), F32), pltpu.SemaphoreType.DMA(())],
        compiler_params=_cparams("arbitrary", "arbitrary"),
        name="combine",
    )(dest.reshape(b * nt, 1, tm * TOP_K), yb, x1, route, mod, g_post)


def _rope_tables(seq):
    half = HEAD_DIM // 2
    nf = half // 2
    freqs = ROPE_THETA ** (-jnp.arange(nf, dtype=F32) / nf)
    t = jnp.arange(seq, dtype=jnp.int32)
    row = (t // GRID_W).astype(F32)[:, None] * freqs
    col = (t % GRID_W).astype(F32)[:, None] * freqs
    cos = jnp.concatenate([jnp.cos(row), jnp.cos(row), jnp.cos(col), jnp.cos(col)], axis=1)
    sin = jnp.concatenate([-jnp.sin(row), jnp.sin(row), -jnp.sin(col), jnp.sin(col)], axis=1)
    reps = LANES // HEAD_DIM
    return jnp.tile(cos, (1, reps)), jnp.tile(sin, (1, reps))


def _routing(top_e, n_tok):
    n_assign = n_tok * TOP_K
    flat_e = top_e.reshape(-1)
    order = jnp.argsort(flat_e)
    e_sorted = flat_e[order]
    counts = jnp.bincount(flat_e, length=N_EXPERTS)
    padded = (counts + EXPERT_BLOCK - 1) // EXPERT_BLOCK * EXPERT_BLOCK
    pad_end = jnp.cumsum(padded)
    pad_start = pad_end - padded
    start = jnp.cumsum(counts) - counts
    dest_sorted = pad_start[e_sorted] + jnp.arange(n_assign, dtype=jnp.int32) - start[e_sorted]
    n_blocks = -(-n_assign // EXPERT_BLOCK) + N_EXPERTS
    n_slots = n_blocks * EXPERT_BLOCK
    slot_tok = jnp.full((n_slots,), n_tok, jnp.int32).at[dest_sorted].set((order // TOP_K).astype(jnp.int32))
    dest = jnp.zeros((n_assign,), jnp.int32).at[order].set(dest_sorted.astype(jnp.int32))
    block_e = jnp.minimum(jnp.searchsorted(pad_end, jnp.arange(n_blocks, dtype=jnp.int32) * EXPERT_BLOCK,
                                           side='right'), N_EXPERTS - 1).astype(jnp.int32)
    nused = (pad_end[-1] // EXPERT_BLOCK).astype(jnp.int32).reshape(1)
    return slot_tok, dest, block_e, nused, n_blocks


def kernel(x, c, ctx, c_ctx, w_mod, b_mod, g_pre_mix, g_post_mix, g_pre_ffn, g_post_ffn, w_in, rpb, g_qnorm,
           g_knorm, w_out_a, w_out_b, w_o, w_router, b_router, w_gu, b_gu, w_dn, b_dn):
    b, s, d = x.shape
    n_ctx = ctx.shape[1]
    assert w_mod.shape[0] == 1, "single layer"
    n_tok = b * s

    cin = jnp.concatenate([c, c_ctx[None, :], jnp.zeros((8 - b - 1, d), F32)], axis=0)
    mod = _modulation(cin, w_mod[0], b_mod[0])
    mod_x = mod[:b].reshape(b, 6, d)
    mod_c = jnp.broadcast_to(mod[b].reshape(1, 6, d), (b, 6, d))

    w_in_bf = w_in[0].astype(BF16)
    cos, sin = _rope_tables(s)
    gq = jnp.tile(g_qnorm[0], GQA_HEADS).reshape(1, GQA_WIDTH)
    gk = jnp.tile(g_knorm[0], GQA_KV_HEADS).reshape(1, GQA_KV_WIDTH)
    head_id = np.arange(GQA_WIDTH) // HEAD_DIM
    ones_bd = jnp.asarray(head_id[:, None] == head_id[None, :], BF16)
    g_pre = g_pre_mix[0].reshape(1, d)

    k_a, v_a, k_b, v_b, q_a, q_b = _inproj(x, mod_x, g_pre, w_in_bf[:, :QKV_COLS], cos, sin, gq, gk, ones_bd,
                                           tm=512, rope=True, with_q=True)
    k_ca, v_ca, k_cb, v_cb = _inproj(ctx, mod_c, g_pre, w_in_bf[:, :KV_COLS], cos[:n_ctx], sin[:n_ctx],
                                           gq, gk, ones_bd, tm=n_ctx, rope=False, with_q=False)

    o_a = _natten(q_a, k_a, v_a, k_ca, v_ca, _na_bias_tables(rpb[0]))

    sk = s + n_ctx
    k_all = jnp.concatenate([k_b, k_cb], axis=1)
    v_all = jnp.concatenate([v_b, v_cb], axis=1)
    kt = k_all.reshape(b, sk // GQA_KC, GQA_KC, GQA_KV_HEADS, HEAD_DIM).transpose(0, 3, 1, 4, 2)
    o_b = _gqa(q_b, kt, v_all.reshape(b, sk // GQA_KC, GQA_KC, GQA_KV_WIDTH))

    wr = jnp.zeros((d, LANES), F32).at[:, :N_EXPERTS].set(w_router[0])
    wrh, wrl = _split_bf16(wr)
    br = jnp.full((1, LANES), NEG_BIG, F32).at[0, :N_EXPERTS].set(b_router[0])
    x1, h2p, route = _merge(x, o_a, o_b, mod_x, g_pre, g_post_mix[0].reshape(1, d), g_pre_ffn[0].reshape(1, d),
                            w_in_bf[:, QKV_COLS:], w_out_a[0].astype(BF16), w_out_b[0].astype(BF16),
                            w_o[0].astype(BF16), wrh, wrl, br, tm=256)

    top_e = route[..., :TOP_K].astype(jnp.int32).reshape(n_tok, TOP_K)
    slot_tok, dest, block_e, nused, n_blocks = _routing(top_e, n_tok)
    xb = _dispatch(slot_tok, nused, h2p.reshape(n_tok, d // 2), n_blocks)
    yb = _experts(block_e, nused, xb, w_gu[0].astype(BF16), b_gu[0], w_dn[0].astype(BF16), b_dn[0])
    return _combine(dest, yb, x1, route, mod_x, g_post_ffn[0].reshape(1, d), tm=128)
```

```python
import functools

import numpy as np
import jax
import jax.numpy as jnp
from jax import lax
from jax.experimental import pallas as pl
from jax.experimental.pallas import tpu as pltpu

F32 = jnp.float32
BF16 = jnp.bfloat16

D_MODEL = 1024
GRID_W = 64
HEAD_DIM = 64
NA_HEADS = 8
NA_WIN_R = 8
NA_WIN_C = 16
GQA_HEADS = 8
GQA_KV_HEADS = 2
ROPE_THETA = 10000.0
N_EXPERTS = 32
TOP_K = 4
D_FF = D_MODEL
SWIGLU_LIMIT = 7.0
SWIGLU_ALPHA = 1.702
EXPERT_BLOCK = 128
NORM_EPS = 1e-6

NA_WIDTH = NA_HEADS * HEAD_DIM
GQA_WIDTH = GQA_HEADS * HEAD_DIM
GQA_KV_WIDTH = GQA_KV_HEADS * HEAD_DIM
KV_COLS = 2 * NA_WIDTH + 2 * GQA_KV_WIDTH
QKV_COLS = KV_COLS + NA_WIDTH + GQA_WIDTH
GATE_COLS = 2 * D_MODEL

LANES = 128
NEG_BIG = -1e30
VMEM_LIMIT = 56 * 1024 * 1024

NA_Q_ROWS = 2
NA_K_ROWS = 10
GQA_TQ = 128
GQA_KC = 256
GQA_CHUNKS_PER_STEP = 2


def _cparams(*sem):
    return pltpu.CompilerParams(dimension_semantics=sem, vmem_limit_bytes=VMEM_LIMIT)


def _split_bf16(a):
    hi = a.astype(BF16)
    lo = (a - hi.astype(F32)).astype(BF16)
    return hi, lo


def _dot(a, b):
    return jnp.dot(a, b, preferred_element_type=F32)


def _dot_nt(a, b):
    return lax.dot_general(a, b, (((1,), (1,)), ((), ())), preferred_element_type=F32)


def _rms(xf):
    return xf * lax.rsqrt(jnp.mean(xf * xf, axis=-1, keepdims=True) + NORM_EPS)


def _mod_kernel(c_ref, whi_ref, wlo_ref, b_ref, o_ref):
    c = c_ref[...]
    s = c * jax.nn.sigmoid(c)
    shi, slo = _split_bf16(s)
    whi = whi_ref[...]
    o_ref[...] = _dot(shi, whi) + _dot(slo, whi) + _dot(shi, wlo_ref[...]) + b_ref[...]


def _modulation(cin, w_mod, b_mod):
    n = w_mod.shape[1]
    bn = 1024
    whi, wlo = _split_bf16(w_mod)
    return pl.pallas_call(
        _mod_kernel,
        grid=(n // bn,),
        in_specs=[pl.BlockSpec((8, D_MODEL), lambda j: (0, 0)),
                  pl.BlockSpec((D_MODEL, bn), lambda j: (0, j)),
                  pl.BlockSpec((D_MODEL, bn), lambda j: (0, j)),
                  pl.BlockSpec((1, bn), lambda j: (0, j))],
        out_specs=pl.BlockSpec((8, bn), lambda j: (0, j)),
        out_shape=jax.ShapeDtypeStruct((8, n), F32),
        compiler_params=_cparams("arbitrary"),
        name="mod",
    )(cin, whi, wlo, b_mod.reshape(1, n))


def _head_norm(y, ones_bd, g):
    hi, lo = _split_bf16(y * y)
    ms = (_dot(hi, ones_bd) + _dot(lo, ones_bd)) * (1.0 / HEAD_DIM)
    return y * lax.rsqrt(ms + NORM_EPS) * g


def _rope(y, cos, sin):
    w = y.shape[1]
    lane = lax.broadcasted_iota(jnp.int32, y.shape, 1)
    up = pltpu.roll(y, w - 16, 1)
    dn = pltpu.roll(y, 16, 1)
    sw = jnp.where((lane % 32) < 16, up, dn)
    return y * cos + sw * sin


def _inproj_kernel(x_ref, mod_ref, g_ref, w_ref, cos_ref, sin_ref, gq_ref, gk_ref, ones_ref,
                   ka_ref, va_ref, kb_ref, vb_ref, *q_refs, rope, with_q):
    scale = HEAD_DIM ** -0.5
    if with_q:
        qa_ref, qb_ref = q_refs
    xf = x_ref[...]
    h = _rms(xf) * g_ref[...] * (1.0 + mod_ref[1:2, :]) + mod_ref[0:1, :]
    y = _dot(h.astype(BF16), w_ref[...])
    ka_ref[...] = y[:, 0:NA_WIDTH].astype(BF16)
    va_ref[...] = y[:, NA_WIDTH:2 * NA_WIDTH].astype(BF16)
    o = 2 * NA_WIDTH
    kb = _head_norm(y[:, o:o + GQA_KV_WIDTH], ones_ref[0:GQA_KV_WIDTH, 0:GQA_KV_WIDTH], gk_ref[...])
    if rope:
        cos = cos_ref[...]
        sin = sin_ref[...]
        kb = _rope(kb, cos, sin)
    kb_ref[...] = kb.astype(BF16)
    vb_ref[...] = y[:, o + GQA_KV_WIDTH:KV_COLS].astype(BF16)
    if with_q:
        qa_ref[...] = (y[:, KV_COLS:KV_COLS + NA_WIDTH] * scale).astype(BF16)
        qb = _head_norm(y[:, KV_COLS + NA_WIDTH:QKV_COLS], ones_ref[...], gq_ref[...])
        if rope:
            reps = GQA_WIDTH // LANES
            qb = _rope(qb, jnp.concatenate([cos] * reps, axis=1), jnp.concatenate([sin] * reps, axis=1))
        qb_ref[...] = (qb * scale).astype(BF16)


def _inproj(x, mod, g_pre, w, cos, sin, gq, gk, ones_bd, *, tm, rope, with_q):
    b, s, d = x.shape
    ncols = w.shape[1]
    tok = lambda i, j: (i, j, 0)
    const = lambda i, j: (0, 0)
    out_w = [NA_WIDTH, NA_WIDTH, GQA_KV_WIDTH, GQA_KV_WIDTH] + ([NA_WIDTH, GQA_WIDTH] if with_q else [])
    kern = functools.partial(_inproj_kernel, rope=rope, with_q=with_q)
    return pl.pallas_call(
        kern,
        grid=(b, s // tm),
        in_specs=[pl.BlockSpec((None, tm, d), tok),
                  pl.BlockSpec((None, 6, d), lambda i, j: (i, 0, 0)),
                  pl.BlockSpec((1, d), const),
                  pl.BlockSpec((d, ncols), const),
                  pl.BlockSpec((tm, LANES), lambda i, j: (j, 0)),
                  pl.BlockSpec((tm, LANES), lambda i, j: (j, 0)),
                  pl.BlockSpec((1, GQA_WIDTH), const),
                  pl.BlockSpec((1, GQA_KV_WIDTH), const),
                  pl.BlockSpec((GQA_WIDTH, GQA_WIDTH), const)],
        out_specs=[pl.BlockSpec((None, tm, wd), tok) for wd in out_w],
        out_shape=[jax.ShapeDtypeStruct((b, s, wd), BF16) for wd in out_w],
        compiler_params=_cparams("arbitrary", "arbitrary"),
        name="inproj_rope" if rope else "inproj_ctx",
    )(x, mod, g_pre, w, cos, sin, gq, gk, ones_bd)


def _na_base(i):
    return jnp.clip(i * NA_Q_ROWS - NA_WIN_R // 2, 0, GRID_W - NA_K_ROWS)


def _na_block_offsets():
    qr = np.arange(GRID_W // NA_Q_ROWS) * NA_Q_ROWS
    base = np.clip(qr - NA_WIN_R // 2, 0, GRID_W - NA_K_ROWS)
    reps = {}
    for q, o in zip(qr, qr - base):
        reps.setdefault(int(o), int(q))
    offs = sorted(reps)
    assert offs == list(range(0, offs[-1] + 1, NA_Q_ROWS)), offs
    return offs, reps


def _na_bias_tables(rpb):
    c = np.arange(GRID_W)
    c0 = np.clip(c - NA_WIN_C // 2, 0, GRID_W - NA_WIN_C)
    col_ok = (c[None, :] >= c0[:, None]) & (c[None, :] < c0[:, None] + NA_WIN_C)
    col_off = c[None, :] - c[:, None] + NA_WIN_C - 1
    onehot = (col_off[None] == np.arange(2 * NA_WIN_C - 1)[:, None, None]) & col_ok[None]
    e = jnp.einsum('hro,ock->hrck', rpb, jnp.asarray(onehot, F32), precision=lax.Precision.HIGHEST)
    e = e + jnp.asarray(np.where(col_ok, 0.0, NEG_BIG), F32)
    neg = jnp.full((NA_HEADS, GRID_W, GRID_W), NEG_BIG, F32)
    offs, reps = _na_block_offsets()
    tabs = []
    for o in offs:
        qr = reps[o]
        base = qr - o
        blk_rows = []
        for rr in range(NA_Q_ROWS):
            r = qr + rr
            r0 = min(max(r - NA_WIN_R // 2, 0), GRID_W - NA_WIN_R)
            pieces = [e[:, kr - r + NA_WIN_R - 1] if r0 <= kr < r0 + NA_WIN_R else neg
                      for kr in range(base, base + NA_K_ROWS)]
            blk_rows.append(jnp.concatenate(pieces, axis=-1))
        tabs.append(jnp.concatenate(blk_rows, axis=1))
    return jnp.stack(tabs)


def _natten_kernel(q_ref, k_ref, v_ref, kc_ref, vc_ref, bias_ref, o_ref):
    i = pl.program_id(1)
    start = pl.multiple_of(_na_base(i) * GRID_W, GRID_W)
    nk = NA_K_ROWS * GRID_W
    slab = 4 * HEAD_DIM
    head_of_lane = lax.broadcasted_iota(jnp.int32, (1, slab), 1) // HEAD_DIM
    for sl in range(NA_HEADS // 4):
        cols = slice(sl * slab, (sl + 1) * slab)
        q4 = q_ref[:, cols]
        k4 = k_ref[pl.ds(start, nk), cols]
        v4 = v_ref[pl.ds(start, nk), cols]
        kc4 = kc_ref[:, cols]
        vc4 = vc_ref[:, cols]
        acc = jnp.zeros((q4.shape[0], slab), F32)
        for g in range(4):
            sel = head_of_lane == g
            qm = jnp.where(sel, q4, jnp.zeros_like(q4))
            s_loc = _dot_nt(qm, k4) + bias_ref[sl * 4 + g]
            s_ctx = _dot_nt(qm, kc4)
            m = jnp.maximum(jnp.max(s_loc, axis=-1, keepdims=True), jnp.max(s_ctx, axis=-1, keepdims=True))
            p_loc = jnp.exp(s_loc - m)
            p_ctx = jnp.exp(s_ctx - m)
            l = jnp.sum(p_loc, axis=-1, keepdims=True) + jnp.sum(p_ctx, axis=-1, keepdims=True)
            o = _dot(p_loc.astype(BF16), v4) + _dot(p_ctx.astype(BF16), vc4)
            acc = acc + jnp.where(sel, o * (1.0 / l), 0.0)
        o_ref[:, cols] = acc.astype(BF16)


def _natten(q_a, k_a, v_a, k_ca, v_ca, bias):
    b, s, w = q_a.shape
    c = k_ca.shape[1]
    tq = NA_Q_ROWS * GRID_W
    nk = NA_K_ROWS * GRID_W
    full = lambda i, j: (i, 0, 0)
    return pl.pallas_call(
        _natten_kernel,
        grid=(b, s // tq),
        in_specs=[pl.BlockSpec((None, tq, w), lambda i, j: (i, j, 0)),
                  pl.BlockSpec((None, s, w), full),
                  pl.BlockSpec((None, s, w), full),
                  pl.BlockSpec((None, c, w), full),
                  pl.BlockSpec((None, c, w), full),
                  pl.BlockSpec((None, NA_HEADS, tq, nk),
                               lambda i, j: ((j * NA_Q_ROWS - _na_base(j)) // NA_Q_ROWS, 0, 0, 0))],
        out_specs=pl.BlockSpec((None, tq, w), lambda i, j: (i, j, 0)),
        out_shape=jax.ShapeDtypeStruct((b, s, w), BF16),
        compiler_params=_cparams("arbitrary", "arbitrary"),
        name="natten",
    )(q_a, k_a, v_a, k_ca, v_ca, bias)


def _gqa_kernel(q_ref, kt_ref, v_ref, o_ref):
    group = GQA_HEADS // GQA_KV_HEADS
    n_chunks = kt_ref.shape[1]
    tq = q_ref.shape[0]
    qs = [jnp.concatenate(
        [q_ref[:, (j * group + g) * HEAD_DIM:(j * group + g + 1) * HEAD_DIM] for g in range(group)], axis=0)
        for j in range(GQA_KV_HEADS)]

    def step(carry, chunks):
        out = []
        for j in range(GQA_KV_HEADS):
            s = jnp.concatenate([_dot(qs[j], kt_ref[j, c]) for c in chunks], axis=1)
            m_new = jnp.max(s, axis=-1, keepdims=True)
            if carry is not None:
                m, acc = carry[j]
                m_new = jnp.maximum(m, m_new)
            p = jnp.exp(s - m_new).astype(BF16)
            pv = _dot(p[:, :GQA_KC], v_ref[j, chunks[0]])
            for n, c in enumerate(chunks[1:], start=1):
                pv = pv + _dot(p[:, n * GQA_KC:(n + 1) * GQA_KC], v_ref[j, c])
            if carry is not None:
                pv = jnp.exp(m - m_new) * acc + pv
            out.append((m_new, pv))
        return tuple(out)

    per = GQA_CHUNKS_PER_STEP
    n_first = per + n_chunks % per
    carry = step(None, list(range(n_first)))
    carry = lax.fori_loop(0, (n_chunks - n_first) // per,
                          lambda i, cr: step(cr, [n_first + i * per + n for n in range(per)]), carry)
    for j in range(GQA_KV_HEADS):
        acc = carry[j][1]
        o = acc[:, :HEAD_DIM] * (1.0 / acc[:, HEAD_DIM:HEAD_DIM + 1])
        for g in range(group):
            h = j * group + g
            o_ref[:, h * HEAD_DIM:(h + 1) * HEAD_DIM] = o[g * tq:(g + 1) * tq].astype(BF16)


def _gqa(q_b, kt, v):
    b, s, w = q_b.shape
    _, nkv, nch, hd, kc = kt.shape
    return pl.pallas_call(
        _gqa_kernel,
        grid=(b, s // GQA_TQ),
        in_specs=[pl.BlockSpec((None, GQA_TQ, w), lambda i, j: (i, j, 0)),
                  pl.BlockSpec((None, nkv, nch, hd, kc), lambda i, j: (i, 0, 0, 0, 0)),
                  pl.BlockSpec((None, nkv, nch, kc, 2 * HEAD_DIM), lambda i, j: (i, 0, 0, 0, 0))],
        out_specs=pl.BlockSpec((None, GQA_TQ, w), lambda i, j: (i, j, 0)),
        out_shape=jax.ShapeDtypeStruct((b, s, w), BF16),
        compiler_params=_cparams("arbitrary", "arbitrary"),
        name="gqa",
    )(q_b, kt, v)


def _merge_kernel(x_ref, oa_ref, ob_ref, mod_ref, g1_ref, g2_ref, g3_ref, wg_ref, woa_ref, wob_ref, wo_ref,
                  wrh_ref, wrl_ref, br_ref, x1_ref, h2_ref, route_ref):
    xf = x_ref[...]
    h = _rms(xf) * g1_ref[...] * (1.0 + mod_ref[1:2, :]) + mod_ref[0:1, :]
    gates = _dot(h.astype(BF16), wg_ref[...])
    ya = _dot(oa_ref[...], woa_ref[...])
    yb = _dot(ob_ref[...], wob_ref[...])
    z = jax.nn.sigmoid(gates[:, :D_MODEL]) * ya + jax.nn.sigmoid(gates[:, D_MODEL:]) * yb
    y = _dot(z.astype(BF16), wo_ref[...])
    x1 = xf + mod_ref[2:3, :] * (_rms(y) * g2_ref[...])
    x1_ref[...] = x1
    h2 = _rms(x1) * g3_ref[...] * (1.0 + mod_ref[4:5, :]) + mod_ref[3:4, :]

    half = D_MODEL // 2
    lo_bits = lax.bitcast_convert_type(h2[:, :half].astype(BF16).astype(F32), jnp.uint32) >> 16
    hi_bits = lax.bitcast_convert_type(h2[:, half:].astype(BF16).astype(F32), jnp.uint32) & jnp.uint32(0xFFFF0000)
    h2_ref[...] = lo_bits | hi_bits

    hh, hl = _split_bf16(h2)
    wrh = wrh_ref[...]
    lg = _dot(hh, wrh) + _dot(hl, wrh) + _dot(hh, wrl_ref[...]) + br_ref[...]
    lane = lax.broadcasted_iota(jnp.int32, lg.shape, 1)
    vals, idxs = [], []
    for _ in range(TOP_K):
        m = jnp.max(lg, axis=-1, keepdims=True)
        idx = jnp.min(jnp.where(lg == m, lane, LANES), axis=-1, keepdims=True)
        vals.append(m)
        idxs.append(idx)
        lg = jnp.where(lane == idx, -3e38, lg)
    es = [jnp.exp(v - vals[0]) for v in vals]
    inv = 1.0 / (es[0] + es[1] + es[2] + es[3])
    route = jnp.zeros(lg.shape, F32)
    for k in range(TOP_K):
        route = jnp.where(lane == k, idxs[k].astype(F32), route)
        route = jnp.where(lane == TOP_K + k, es[k] * inv, route)
    route_ref[...] = route


def _merge(x, o_a, o_b, mod, g1, g2, g3, wg, woa, wob, wo, wrh, wrl, br, *, tm):
    b, s, d = x.shape
    tok = lambda i, j: (i, j, 0)
    const = lambda i, j: (0, 0)
    return pl.pallas_call(
        _merge_kernel,
        grid=(b, s // tm),
        in_specs=[pl.BlockSpec((None, tm, d), tok),
                  pl.BlockSpec((None, tm, NA_WIDTH), tok),
                  pl.BlockSpec((None, tm, GQA_WIDTH), tok),
                  pl.BlockSpec((None, 6, d), lambda i, j: (i, 0, 0)),
                  pl.BlockSpec((1, d), const), pl.BlockSpec((1, d), const), pl.BlockSpec((1, d), const),
                  pl.BlockSpec((d, GATE_COLS), const),
                  pl.BlockSpec((NA_WIDTH, d), const),
                  pl.BlockSpec((GQA_WIDTH, d), const),
                  pl.BlockSpec((d, d), const),
                  pl.BlockSpec((d, LANES), const), pl.BlockSpec((d, LANES), const),
                  pl.BlockSpec((1, LANES), const)],
        out_specs=[pl.BlockSpec((None, tm, d), tok),
                   pl.BlockSpec((None, tm, d // 2), tok),
                   pl.BlockSpec((None, tm, LANES), tok)],
        out_shape=[jax.ShapeDtypeStruct((b, s, d), F32),
                   jax.ShapeDtypeStruct((b, s, d // 2), jnp.uint32),
                   jax.ShapeDtypeStruct((b, s, LANES), F32)],
        compiler_params=_cparams("arbitrary", "arbitrary"),
        name="merge",
    )(x, o_a, o_b, mod, g1, g2, g3, wg, woa, wob, wo, wrh, wrl, br)


def _row_copy(src_hbm, dst_vmem, sem, row, i):
    return pltpu.make_async_copy(src_hbm.at[pl.ds(row, 1)], dst_vmem.at[pl.ds(i, 1)], sem)


def _dispatch_kernel(tok_ref, nused_ref, h2_hbm, o_ref, sem, *, n_tok):
    blk = pl.program_id(0)
    rows = o_ref.shape[0]

    @pl.when(blk < nused_ref[0])
    def _():
        def issue(i, carry):
            t = tok_ref[0, i]

            @pl.when(t < n_tok)
            def _():
                _row_copy(h2_hbm, o_ref, sem, t, i).start()

            @pl.when(t >= n_tok)
            def _():
                o_ref[pl.ds(i, 1), :] = jnp.zeros((1, o_ref.shape[1]), o_ref.dtype)
            return carry

        lax.fori_loop(0, rows, issue, 0)

        def drain(i, carry):
            @pl.when(tok_ref[0, i] < n_tok)
            def _():
                _row_copy(h2_hbm, o_ref, sem, 0, i).wait()
            return carry

        lax.fori_loop(0, rows, drain, 0)

    @pl.when(blk >= nused_ref[0])
    def _():
        o_ref[...] = jnp.zeros(o_ref.shape, o_ref.dtype)


def _dispatch(slot_tok, nused, h2p, n_blocks):
    n_tok, wd = h2p.shape
    kern = functools.partial(_dispatch_kernel, n_tok=n_tok)
    return pl.pallas_call(
        kern,
        grid=(n_blocks,),
        in_specs=[pl.BlockSpec((None, 1, EXPERT_BLOCK), lambda i: (i, 0, 0), memory_space=pltpu.SMEM),
                  pl.BlockSpec(memory_space=pltpu.SMEM),
                  pl.BlockSpec(memory_space=pl.ANY)],
        out_specs=pl.BlockSpec((EXPERT_BLOCK, wd), lambda i: (i, 0)),
        out_shape=jax.ShapeDtypeStruct((n_blocks * EXPERT_BLOCK, wd), jnp.uint32),
        scratch_shapes=[pltpu.SemaphoreType.DMA(())],
        compiler_params=_cparams("arbitrary"),
        name="dispatch",
    )(slot_tok.reshape(n_blocks, 1, EXPERT_BLOCK), nused, h2p)


def _expert_kernel(be_ref, nused_ref, x_ref, wgu_ref, bgu_ref, wdn_ref, bdn_ref, o_ref):
    blk = pl.program_id(0)

    @pl.when(blk < nused_ref[0])
    def _():
        p = x_ref[...]
        xa = lax.bitcast_convert_type(p << 16, F32)
        xb = lax.bitcast_convert_type(p & jnp.uint32(0xFFFF0000), F32)
        x = jnp.concatenate([xa, xb], axis=1).astype(BF16)
        gu = _dot(x, wgu_ref[...]) + bgu_ref[...]
        x_glu = jnp.minimum(gu[:, :D_FF], SWIGLU_LIMIT)
        x_lin = jnp.clip(gu[:, D_FF:], -SWIGLU_LIMIT, SWIGLU_LIMIT)
        act = (x_lin + 1.0) * (x_glu * jax.nn.sigmoid(SWIGLU_ALPHA * x_glu))
        o_ref[...] = _dot(act.astype(BF16), wdn_ref[...]) + bdn_ref[...]

    @pl.when(blk >= nused_ref[0])
    def _():
        o_ref[...] = jnp.zeros(o_ref.shape, o_ref.dtype)


def _experts(block_e, nused, xb, w_gu, b_gu, w_dn, b_dn):
    n_slots, wd = xb.shape
    n_blocks = n_slots // EXPERT_BLOCK
    grid_spec = pltpu.PrefetchScalarGridSpec(
        num_scalar_prefetch=2,
        grid=(n_blocks,),
        in_specs=[pl.BlockSpec((EXPERT_BLOCK, wd), lambda i, be, nu: (i, 0)),
                  pl.BlockSpec((None, D_MODEL, 2 * D_FF), lambda i, be, nu: (be[i], 0, 0)),
                  pl.BlockSpec((None, 1, 2 * D_FF), lambda i, be, nu: (be[i], 0, 0)),
                  pl.BlockSpec((None, D_FF, D_MODEL), lambda i, be, nu: (be[i], 0, 0)),
                  pl.BlockSpec((None, 1, D_MODEL), lambda i, be, nu: (be[i], 0, 0))],
        out_specs=pl.BlockSpec((EXPERT_BLOCK, D_MODEL), lambda i, be, nu: (i, 0)),
    )
    return pl.pallas_call(
        _expert_kernel,
        grid_spec=grid_spec,
        out_shape=jax.ShapeDtypeStruct((n_slots, D_MODEL), F32),
        compiler_params=_cparams("arbitrary"),
        name="experts",
    )(block_e, nused, xb, w_gu, b_gu.reshape(N_EXPERTS, 1, 2 * D_FF), w_dn, b_dn.reshape(N_EXPERTS, 1, D_MODEL))


def _combine_kernel(dest_ref, yb_hbm, x1_ref, w_ref, mod_ref, g_ref, o_ref, buf, sem):
    tm = x1_ref.shape[0]

    def issue(i, carry):
        for k in range(TOP_K):
            pltpu.make_async_copy(yb_hbm.at[pl.ds(dest_ref[0, i * TOP_K + k], 1)],
                                  buf.at[k, pl.ds(i, 1)], sem).start()
        return carry

    lax.fori_loop(0, tm, issue, 0)

    def drain(i, carry):
        for k in range(TOP_K):
            pltpu.make_async_copy(yb_hbm.at[pl.ds(0, 1)], buf.at[k, pl.ds(i, 1)], sem).wait()
        return carry

    lax.fori_loop(0, tm, drain, 0)

    w = w_ref[...]
    f = buf[0] * w[:, TOP_K:TOP_K + 1]
    for k in range(1, TOP_K):
        f = f + buf[k] * w[:, TOP_K + k:TOP_K + k + 1]
    o_ref[...] = x1_ref[...] + mod_ref[5:6, :] * (_rms(f) * g_ref[...])


def _combine(dest, yb, x1, route, mod, g_post, *, tm):
    b, s, d = x1.shape
    nt = s // tm
    tok = lambda i, j: (i, j, 0)
    return pl.pallas_call(
        _combine_kernel,
        grid=(b, nt),
        in_specs=[pl.BlockSpec((None, 1, tm * TOP_K), lambda i, j: (i * nt + j, 0, 0), memory_space=pltpu.SMEM),
                  pl.BlockSpec(memory_space=pl.ANY),
                  pl.BlockSpec((None, tm, d), tok),
                  pl.BlockSpec((None, tm, LANES), tok),
                  pl.BlockSpec((None, 6, d), lambda i, j: (i, 0, 0)),
                  pl.BlockSpec((1, d), lambda i, j: (0, 0))],
        out_specs=pl.BlockSpec((None, tm, d), tok),
        out_shape=jax.ShapeDtypeStruct((b, s, d), F32),
        scratch_shapes=[pltpu.VMEM((TOP_K, tm, d), F32), pltpu.SemaphoreType.DMA(())],
        compiler_params=_cparams("arbitrary", "arbitrary"),
        name="combine",
    )(dest.reshape(b * nt, 1, tm * TOP_K), yb, x1, route, mod, g_post)


def _rope_tables(seq):
    half = HEAD_DIM // 2
    nf = half // 2
    freqs = ROPE_THETA ** (-jnp.arange(nf, dtype=F32) / nf)
    t = jnp.arange(seq, dtype=jnp.int32)
    row = (t // GRID_W).astype(F32)[:, None] * freqs
    col = (t % GRID_W).astype(F32)[:, None] * freqs
    cos = jnp.concatenate([jnp.cos(row), jnp.cos(row), jnp.cos(col), jnp.cos(col)], axis=1)
    sin = jnp.concatenate([-jnp.sin(row), jnp.sin(row), -jnp.sin(col), jnp.sin(col)], axis=1)
    reps = LANES // HEAD_DIM
    return jnp.tile(cos, (1, reps)), jnp.tile(sin, (1, reps))


def _routing(top_e, n_tok):
    n_assign = n_tok * TOP_K
    flat_e = top_e.reshape(-1)
    order = jnp.argsort(flat_e)
    e_sorted = flat_e[order]
    counts = jnp.bincount(flat_e, length=N_EXPERTS)
    padded = (counts + EXPERT_BLOCK - 1) // EXPERT_BLOCK * EXPERT_BLOCK
    pad_end = jnp.cumsum(padded)
    pad_start = pad_end - padded
    start = jnp.cumsum(counts) - counts
    dest_sorted = pad_start[e_sorted] + jnp.arange(n_assign, dtype=jnp.int32) - start[e_sorted]
    n_blocks = -(-n_assign // EXPERT_BLOCK) + N_EXPERTS
    n_slots = n_blocks * EXPERT_BLOCK
    slot_tok = jnp.full((n_slots,), n_tok, jnp.int32).at[dest_sorted].set((order // TOP_K).astype(jnp.int32))
    dest = jnp.zeros((n_assign,), jnp.int32).at[order].set(dest_sorted.astype(jnp.int32))
    block_e = jnp.minimum(jnp.searchsorted(pad_end, jnp.arange(n_blocks, dtype=jnp.int32) * EXPERT_BLOCK,
                                           side='right'), N_EXPERTS - 1).astype(jnp.int32)
    nused = (pad_end[-1] // EXPERT_BLOCK).astype(jnp.int32).reshape(1)
    return slot_tok, dest, block_e, nused, n_blocks


def kernel(x, c, ctx, c_ctx, w_mod, b_mod, g_pre_mix, g_post_mix, g_pre_ffn, g_post_ffn, w_in, rpb, g_qnorm,
           g_knorm, w_out_a, w_out_b, w_o, w_router, b_router, w_gu, b_gu, w_dn, b_dn):
    b, s, d = x.shape
    n_ctx = ctx.shape[1]
    assert w_mod.shape[0] == 1, "single layer"
    n_tok = b * s

    cin = jnp.concatenate([c, c_ctx[None, :], jnp.zeros((8 - b - 1, d), F32)], axis=0)
    mod = _modulation(cin, w_mod[0], b_mod[0])
    mod_x = mod[:b].reshape(b, 6, d)
    mod_c = jnp.broadcast_to(mod[b].reshape(1, 6, d), (b, 6, d))

    w_in_bf = w_in[0].astype(BF16)
    cos, sin = _rope_tables(s)
    gq = jnp.tile(g_qnorm[0], GQA_HEADS).reshape(1, GQA_WIDTH)
    gk = jnp.tile(g_knorm[0], GQA_KV_HEADS).reshape(1, GQA_KV_WIDTH)
    head_id = np.arange(GQA_WIDTH) // HEAD_DIM
    ones_bd = jnp.asarray(head_id[:, None] == head_id[None, :], BF16)
    g_pre = g_pre_mix[0].reshape(1, d)

    k_a, v_a, k_b, v_b, q_a, q_b = _inproj(x, mod_x, g_pre, w_in_bf[:, :QKV_COLS], cos, sin, gq, gk, ones_bd,
                                           tm=512, rope=True, with_q=True)
    k_ca, v_ca, k_cb, v_cb = _inproj(ctx, mod_c, g_pre, w_in_bf[:, :KV_COLS], cos[:n_ctx], sin[:n_ctx],
                                           gq, gk, ones_bd, tm=n_ctx, rope=False, with_q=False)

    o_a = _natten(q_a, k_a, v_a, k_ca, v_ca, _na_bias_tables(rpb[0]))

    sk = s + n_ctx
    k_all = jnp.concatenate([k_b, k_cb], axis=1)
    v_all = jnp.concatenate([v_b, v_cb], axis=1)
    nch = sk // GQA_KC
    kt = k_all.reshape(b, nch, GQA_KC, GQA_KV_HEADS, HEAD_DIM).transpose(0, 3, 1, 4, 2)
    v_heads = v_all.reshape(b, nch, GQA_KC, GQA_KV_HEADS, HEAD_DIM).transpose(0, 3, 1, 2, 4)
    v_ext = jnp.concatenate([v_heads, jnp.ones_like(v_heads)], axis=-1)
    o_b = _gqa(q_b, kt, v_ext)

    wr = jnp.zeros((d, LANES), F32).at[:, :N_EXPERTS].set(w_router[0])
    wrh, wrl = _split_bf16(wr)
    br = jnp.full((1, LANES), NEG_BIG, F32).at[0, :N_EXPERTS].set(b_router[0])
    x1, h2p, route = _merge(x, o_a, o_b, mod_x, g_pre, g_post_mix[0].reshape(1, d), g_pre_ffn[0].reshape(1, d),
                            w_in_bf[:, QKV_COLS:], w_out_a[0].astype(BF16), w_out_b[0].astype(BF16),
                            w_o[0].astype(BF16), wrh, wrl, br, tm=256)

    top_e = route[..., :TOP_K].astype(jnp.int32).reshape(n_tok, TOP_K)
    slot_tok, dest, block_e, nused, n_blocks = _routing(top_e, n_tok)
    xb = _dispatch(slot_tok, nused, h2p.reshape(n_tok, d // 2), n_blocks)
    yb = _experts(block_e, nused, xb, w_gu[0].astype(BF16), b_gu[0], w_dn[0].astype(BF16), b_dn[0])
    return _combine(dest, yb, x1, route, mod_x, g_post_ffn[0].reshape(1, d), tm=128)
```

```python
import functools

import numpy as np
import jax
import jax.numpy as jnp
from jax import lax
from jax.experimental import pallas as pl
from jax.experimental.pallas import tpu as pltpu

F32 = jnp.float32
BF16 = jnp.bfloat16

D_MODEL = 1024
GRID_W = 64
HEAD_DIM = 64
NA_HEADS = 8
NA_WIN_R = 8
NA_WIN_C = 16
GQA_HEADS = 8
GQA_KV_HEADS = 2
ROPE_THETA = 10000.0
N_EXPERTS = 32
TOP_K = 4
D_FF = D_MODEL
SWIGLU_LIMIT = 7.0
SWIGLU_ALPHA = 1.702
EXPERT_BLOCK = 128
NORM_EPS = 1e-6

NA_WIDTH = NA_HEADS * HEAD_DIM
GQA_WIDTH = GQA_HEADS * HEAD_DIM
GQA_KV_WIDTH = GQA_KV_HEADS * HEAD_DIM
KV_COLS = 2 * NA_WIDTH + 2 * GQA_KV_WIDTH
QKV_COLS = KV_COLS + NA_WIDTH + GQA_WIDTH
GATE_COLS = 2 * D_MODEL

LANES = 128
NEG_BIG = -1e30
VMEM_LIMIT = 56 * 1024 * 1024

NA_Q_ROWS = 2
NA_K_ROWS = 10
GQA_TQ = 128
GQA_KC = 256
GQA_CHUNKS_PER_STEP = 2


def _cparams(*sem):
    return pltpu.CompilerParams(dimension_semantics=sem, vmem_limit_bytes=VMEM_LIMIT)


def _split_bf16(a):
    hi = a.astype(BF16)
    lo = (a - hi.astype(F32)).astype(BF16)
    return hi, lo


def _dot(a, b):
    return jnp.dot(a, b, preferred_element_type=F32)


def _dot_nt(a, b):
    return lax.dot_general(a, b, (((1,), (1,)), ((), ())), preferred_element_type=F32)


def _rms(xf):
    return xf * lax.rsqrt(jnp.mean(xf * xf, axis=-1, keepdims=True) + NORM_EPS)


def _mod_kernel(c_ref, whi_ref, wlo_ref, b_ref, o_ref):
    c = c_ref[...]
    s = c * jax.nn.sigmoid(c)
    shi, slo = _split_bf16(s)
    whi = whi_ref[...]
    o_ref[...] = _dot(shi, whi) + _dot(slo, whi) + _dot(shi, wlo_ref[...]) + b_ref[...]


def _modulation(cin, w_mod, b_mod):
    n = w_mod.shape[1]
    bn = 1024
    whi, wlo = _split_bf16(w_mod)
    return pl.pallas_call(
        _mod_kernel,
        grid=(n // bn,),
        in_specs=[pl.BlockSpec((8, D_MODEL), lambda j: (0, 0)),
                  pl.BlockSpec((D_MODEL, bn), lambda j: (0, j)),
                  pl.BlockSpec((D_MODEL, bn), lambda j: (0, j)),
                  pl.BlockSpec((1, bn), lambda j: (0, j))],
        out_specs=pl.BlockSpec((8, bn), lambda j: (0, j)),
        out_shape=jax.ShapeDtypeStruct((8, n), F32),
        compiler_params=_cparams("arbitrary"),
        name="mod",
    )(cin, whi, wlo, b_mod.reshape(1, n))


def _head_norm(y, ones_bd, g):
    hi, lo = _split_bf16(y * y)
    ms = (_dot(hi, ones_bd) + _dot(lo, ones_bd)) * (1.0 / HEAD_DIM)
    return y * lax.rsqrt(ms + NORM_EPS) * g


def _rope(y, cos, sin):
    w = y.shape[1]
    lane = lax.broadcasted_iota(jnp.int32, y.shape, 1)
    up = pltpu.roll(y, w - 16, 1)
    dn = pltpu.roll(y, 16, 1)
    sw = jnp.where((lane % 32) < 16, up, dn)
    return y * cos + sw * sin


def _inproj_kernel(x_ref, mod_ref, g_ref, w_ref, cos_ref, sin_ref, gq_ref, gk_ref, ones_ref,
                   ka_ref, va_ref, kb_ref, vb_ref, *q_refs, rope, with_q):
    scale = HEAD_DIM ** -0.5
    if with_q:
        qa_ref, qb_ref = q_refs
    xf = x_ref[...]
    h = _rms(xf) * g_ref[...] * (1.0 + mod_ref[1:2, :]) + mod_ref[0:1, :]
    y = _dot(h.astype(BF16), w_ref[...])
    ka_ref[...] = y[:, 0:NA_WIDTH].astype(BF16)
    va_ref[...] = y[:, NA_WIDTH:2 * NA_WIDTH].astype(BF16)
    o = 2 * NA_WIDTH
    kb = _head_norm(y[:, o:o + GQA_KV_WIDTH], ones_ref[0:GQA_KV_WIDTH, 0:GQA_KV_WIDTH], gk_ref[...])
    if rope:
        cos = cos_ref[...]
        sin = sin_ref[...]
        kb = _rope(kb, cos, sin)
    kb_ref[...] = kb.astype(BF16)
    vb_ref[...] = y[:, o + GQA_KV_WIDTH:KV_COLS].astype(BF16)
    if with_q:
        qa_ref[...] = (y[:, KV_COLS:KV_COLS + NA_WIDTH] * scale).astype(BF16)
        qb = _head_norm(y[:, KV_COLS + NA_WIDTH:QKV_COLS], ones_ref[...], gq_ref[...])
        if rope:
            reps = GQA_WIDTH // LANES
            qb = _rope(qb, jnp.concatenate([cos] * reps, axis=1), jnp.concatenate([sin] * reps, axis=1))
        qb_ref[...] = (qb * scale).astype(BF16)


def _inproj(x, mod, g_pre, w, cos, sin, gq, gk, ones_bd, *, tm, rope, with_q):
    b, s, d = x.shape
    ncols = w.shape[1]
    tok = lambda i, j: (i, j, 0)
    const = lambda i, j: (0, 0)
    out_w = [NA_WIDTH, NA_WIDTH, GQA_KV_WIDTH, GQA_KV_WIDTH] + ([NA_WIDTH, GQA_WIDTH] if with_q else [])
    kern = functools.partial(_inproj_kernel, rope=rope, with_q=with_q)
    return pl.pallas_call(
        kern,
        grid=(b, s // tm),
        in_specs=[pl.BlockSpec((None, tm, d), tok),
                  pl.BlockSpec((None, 6, d), lambda i, j: (i, 0, 0)),
                  pl.BlockSpec((1, d), const),
                  pl.BlockSpec((d, ncols), const),
                  pl.BlockSpec((tm, LANES), lambda i, j: (j, 0)),
                  pl.BlockSpec((tm, LANES), lambda i, j: (j, 0)),
                  pl.BlockSpec((1, GQA_WIDTH), const),
                  pl.BlockSpec((1, GQA_KV_WIDTH), const),
                  pl.BlockSpec((GQA_WIDTH, GQA_WIDTH), const)],
        out_specs=[pl.BlockSpec((None, tm, wd), tok) for wd in out_w],
        out_shape=[jax.ShapeDtypeStruct((b, s, wd), BF16) for wd in out_w],
        compiler_params=_cparams("arbitrary", "arbitrary"),
        name="inproj_rope" if rope else "inproj_ctx",
    )(x, mod, g_pre, w, cos, sin, gq, gk, ones_bd)


def _na_base(i):
    return jnp.clip(i * NA_Q_ROWS - NA_WIN_R // 2, 0, GRID_W - NA_K_ROWS)


def _na_block_offsets():
    qr = np.arange(GRID_W // NA_Q_ROWS) * NA_Q_ROWS
    base = np.clip(qr - NA_WIN_R // 2, 0, GRID_W - NA_K_ROWS)
    reps = {}
    for q, o in zip(qr, qr - base):
        reps.setdefault(int(o), int(q))
    offs = sorted(reps)
    assert offs == list(range(0, offs[-1] + 1, NA_Q_ROWS)), offs
    return offs, reps


def _na_bias_tables(rpb):
    c = np.arange(GRID_W)
    c0 = np.clip(c - NA_WIN_C // 2, 0, GRID_W - NA_WIN_C)
    col_ok = (c[None, :] >= c0[:, None]) & (c[None, :] < c0[:, None] + NA_WIN_C)
    col_off = c[None, :] - c[:, None] + NA_WIN_C - 1
    onehot = (col_off[None] == np.arange(2 * NA_WIN_C - 1)[:, None, None]) & col_ok[None]
    e = jnp.einsum('hro,ock->hrck', rpb, jnp.asarray(onehot, F32), precision=lax.Precision.HIGHEST)
    e = e + jnp.asarray(np.where(col_ok, 0.0, NEG_BIG), F32)
    neg = jnp.full((NA_HEADS, GRID_W, GRID_W), NEG_BIG, F32)
    offs, reps = _na_block_offsets()
    tabs = []
    for o in offs:
        qr = reps[o]
        base = qr - o
        blk_rows = []
        for rr in range(NA_Q_ROWS):
            r = qr + rr
            r0 = min(max(r - NA_WIN_R // 2, 0), GRID_W - NA_WIN_R)
            pieces = [e[:, kr - r + NA_WIN_R - 1] if r0 <= kr < r0 + NA_WIN_R else neg
                      for kr in range(base, base + NA_K_ROWS)]
            blk_rows.append(jnp.concatenate(pieces, axis=-1))
        tabs.append(jnp.concatenate(blk_rows, axis=1))
    return jnp.stack(tabs)


def _natten_kernel(q_ref, k_ref, v_ref, kc_ref, vc_ref, bias_ref, o_ref):
    i = pl.program_id(1)
    start = pl.multiple_of(_na_base(i) * GRID_W, GRID_W)
    nk = NA_K_ROWS * GRID_W
    slab = 4 * HEAD_DIM
    head_of_lane = lax.broadcasted_iota(jnp.int32, (1, slab), 1) // HEAD_DIM
    for sl in range(NA_HEADS // 4):
        cols = slice(sl * slab, (sl + 1) * slab)
        q4 = q_ref[:, cols]
        k4 = k_ref[pl.ds(start, nk), cols]
        v4 = v_ref[pl.ds(start, nk), cols]
        kc4 = kc_ref[:, cols]
        vc4 = vc_ref[:, cols]
        acc = jnp.zeros((q4.shape[0], slab), F32)
        for g in range(4):
            sel = head_of_lane == g
            qm = jnp.where(sel, q4, jnp.zeros_like(q4))
            s_loc = _dot_nt(qm, k4) + bias_ref[sl * 4 + g]
            s_ctx = _dot_nt(qm, kc4)
            m = jnp.maximum(jnp.max(s_loc, axis=-1, keepdims=True), jnp.max(s_ctx, axis=-1, keepdims=True))
            p_loc = jnp.exp(s_loc - m)
            p_ctx = jnp.exp(s_ctx - m)
            l = jnp.sum(p_loc, axis=-1, keepdims=True) + jnp.sum(p_ctx, axis=-1, keepdims=True)
            o = _dot(p_loc.astype(BF16), v4) + _dot(p_ctx.astype(BF16), vc4)
            acc = acc + jnp.where(sel, o * (1.0 / l), 0.0)
        o_ref[:, cols] = acc.astype(BF16)


def _natten(q_a, k_a, v_a, k_ca, v_ca, bias):
    b, s, w = q_a.shape
    c = k_ca.shape[1]
    tq = NA_Q_ROWS * GRID_W
    nk = NA_K_ROWS * GRID_W
    full = lambda i, j: (i, 0, 0)
    return pl.pallas_call(
        _natten_kernel,
        grid=(b, s // tq),
        in_specs=[pl.BlockSpec((None, tq, w), lambda i, j: (i, j, 0)),
                  pl.BlockSpec((None, s, w), full),
                  pl.BlockSpec((None, s, w), full),
                  pl.BlockSpec((None, c, w), full),
                  pl.BlockSpec((None, c, w), full),
                  pl.BlockSpec((None, NA_HEADS, tq, nk),
                               lambda i, j: ((j * NA_Q_ROWS - _na_base(j)) // NA_Q_ROWS, 0, 0, 0))],
        out_specs=pl.BlockSpec((None, tq, w), lambda i, j: (i, j, 0)),
        out_shape=jax.ShapeDtypeStruct((b, s, w), BF16),
        compiler_params=_cparams("arbitrary", "arbitrary"),
        name="natten",
    )(q_a, k_a, v_a, k_ca, v_ca, bias)


def _gqa_kernel(q_ref, kt_ref, v_ref, o_ref):
    group = GQA_HEADS // GQA_KV_HEADS
    n_chunks = kt_ref.shape[1]
    tq = q_ref.shape[0]
    qs = [jnp.concatenate(
        [q_ref[:, (j * group + g) * HEAD_DIM:(j * group + g + 1) * HEAD_DIM] for g in range(group)], axis=0)
        for j in range(GQA_KV_HEADS)]

    def step(carry, chunks):
        out = []
        for j in range(GQA_KV_HEADS):
            s = jnp.concatenate([_dot(qs[j], kt_ref[j, c]) for c in chunks], axis=1)
            m_new = jnp.max(s, axis=-1, keepdims=True)
            if carry is not None:
                m, acc = carry[j]
                m_new = jnp.maximum(m, m_new)
            p = jnp.exp(s - m_new).astype(BF16)
            pv = _dot(p[:, :GQA_KC], v_ref[j, chunks[0]])
            for n, c in enumerate(chunks[1:], start=1):
                pv = pv + _dot(p[:, n * GQA_KC:(n + 1) * GQA_KC], v_ref[j, c])
            if carry is not None:
                pv = jnp.exp(m - m_new) * acc + pv
            out.append((m_new, pv))
        return tuple(out)

    per = GQA_CHUNKS_PER_STEP
    n_first = per + n_chunks % per
    carry = step(None, list(range(n_first)))
    carry = lax.fori_loop(0, (n_chunks - n_first) // per,
                          lambda i, cr: step(cr, [n_first + i * per + n for n in range(per)]), carry)
    for j in range(GQA_KV_HEADS):
        acc = carry[j][1]
        o = acc[:, :HEAD_DIM] * (1.0 / acc[:, HEAD_DIM:HEAD_DIM + 1])
        for g in range(group):
            h = j * group + g
            o_ref[:, h * HEAD_DIM:(h + 1) * HEAD_DIM] = o[g * tq:(g + 1) * tq].astype(BF16)


def _gqa(q_b, kt, v):
    b, s, w = q_b.shape
    _, nkv, nch, hd, kc = kt.shape
    return pl.pallas_call(
        _gqa_kernel,
        grid=(b, s // GQA_TQ),
        in_specs=[pl.BlockSpec((None, GQA_TQ, w), lambda i, j: (i, j, 0)),
                  pl.BlockSpec((None, nkv, nch, hd, kc), lambda i, j: (i, 0, 0, 0, 0)),
                  pl.BlockSpec((None, nkv, nch, kc, 2 * HEAD_DIM), lambda i, j: (i, 0, 0, 0, 0))],
        out_specs=pl.BlockSpec((None, GQA_TQ, w), lambda i, j: (i, j, 0)),
        out_shape=jax.ShapeDtypeStruct((b, s, w), BF16),
        compiler_params=_cparams("arbitrary", "arbitrary"),
        name="gqa",
    )(q_b, kt, v)


def _merge_kernel(x_ref, oa_ref, ob_ref, mod_ref, g1_ref, g2_ref, g3_ref, wg_ref, woa_ref, wob_ref, wo_ref,
                  wrh_ref, wrl_ref, br_ref, x1_ref, h2_ref, route_ref):
    xf = x_ref[...]
    h = _rms(xf) * g1_ref[...] * (1.0 + mod_ref[1:2, :]) + mod_ref[0:1, :]
    gates = _dot(h.astype(BF16), wg_ref[...])
    ya = _dot(oa_ref[...], woa_ref[...])
    yb = _dot(ob_ref[...], wob_ref[...])
    z = jax.nn.sigmoid(gates[:, :D_MODEL]) * ya + jax.nn.sigmoid(gates[:, D_MODEL:]) * yb
    y = _dot(z.astype(BF16), wo_ref[...])
    x1 = xf + mod_ref[2:3, :] * (_rms(y) * g2_ref[...])
    x1_ref[...] = x1
    h2 = _rms(x1) * g3_ref[...] * (1.0 + mod_ref[4:5, :]) + mod_ref[3:4, :]

    half = D_MODEL // 2
    lo_bits = lax.bitcast_convert_type(h2[:, :half].astype(BF16).astype(F32), jnp.uint32) >> 16
    hi_bits = lax.bitcast_convert_type(h2[:, half:].astype(BF16).astype(F32), jnp.uint32) & jnp.uint32(0xFFFF0000)
    h2_ref[...] = lo_bits | hi_bits

    hh, hl = _split_bf16(h2)
    wrh = wrh_ref[...]
    lg = _dot(hh, wrh) + _dot(hl, wrh) + _dot(hh, wrl_ref[...]) + br_ref[...]
    lane = lax.broadcasted_iota(jnp.int32, lg.shape, 1)
    vals, idxs = [], []
    for _ in range(TOP_K):
        m = jnp.max(lg, axis=-1, keepdims=True)
        idx = jnp.min(jnp.where(lg == m, lane, LANES), axis=-1, keepdims=True)
        vals.append(m)
        idxs.append(idx)
        lg = jnp.where(lane == idx, -3e38, lg)
    es = [jnp.exp(v - vals[0]) for v in vals]
    inv = 1.0 / (es[0] + es[1] + es[2] + es[3])
    route = jnp.zeros(lg.shape, F32)
    for k in range(TOP_K):
        route = jnp.where(lane == k, idxs[k].astype(F32), route)
        route = jnp.where(lane == TOP_K + k, es[k] * inv, route)
    route_ref[...] = route


def _merge(x, o_a, o_b, mod, g1, g2, g3, wg, woa, wob, wo, wrh, wrl, br, *, tm):
    b, s, d = x.shape
    tok = lambda i, j: (i, j, 0)
    const = lambda i, j: (0, 0)
    return pl.pallas_call(
        _merge_kernel,
        grid=(b, s // tm),
        in_specs=[pl.BlockSpec((None, tm, d), tok),
                  pl.BlockSpec((None, tm, NA_WIDTH), tok),
                  pl.BlockSpec((None, tm, GQA_WIDTH), tok),
                  pl.BlockSpec((None, 6, d), lambda i, j: (i, 0, 0)),
                  pl.BlockSpec((1, d), const), pl.BlockSpec((1, d), const), pl.BlockSpec((1, d), const),
                  pl.BlockSpec((d, GATE_COLS), const),
                  pl.BlockSpec((NA_WIDTH, d), const),
                  pl.BlockSpec((GQA_WIDTH, d), const),
                  pl.BlockSpec((d, d), const),
                  pl.BlockSpec((d, LANES), const), pl.BlockSpec((d, LANES), const),
                  pl.BlockSpec((1, LANES), const)],
        out_specs=[pl.BlockSpec((None, tm, d), tok),
                   pl.BlockSpec((None, tm, d // 2), tok),
                   pl.BlockSpec((None, tm, LANES), tok)],
        out_shape=[jax.ShapeDtypeStruct((b, s, d), F32),
                   jax.ShapeDtypeStruct((b, s, d // 2), jnp.uint32),
                   jax.ShapeDtypeStruct((b, s, LANES), F32)],
        compiler_params=_cparams("arbitrary", "arbitrary"),
        name="merge",
    )(x, o_a, o_b, mod, g1, g2, g3, wg, woa, wob, wo, wrh, wrl, br)


def _expert_kernel(be_ref, nused_ref, tok_ref, tok_next_ref, dst_prev_ref, dst_ref, h2_hbm, wgu_ref, bgu_ref,
                   wdn_ref, bdn_ref, y_hbm, xbuf, ybuf, gsem, ssem):
    blk = pl.program_id(0)
    nused = nused_ref[0]
    slot = blk % 2
    rows = EXPERT_BLOCK

    def gather(idx_ref, s):
        for i in range(rows):
            pltpu.make_async_copy(h2_hbm.at[pl.ds(idx_ref[0, i], 1)], xbuf.at[s, pl.ds(i, 1)], gsem.at[s]).start()

    def gather_wait(s):
        for i in range(rows):
            pltpu.make_async_copy(h2_hbm.at[pl.ds(0, 1)], xbuf.at[s, pl.ds(i, 1)], gsem.at[s]).wait()

    def scatter(idx_ref, s):
        for i in range(rows):
            pltpu.make_async_copy(ybuf.at[s, pl.ds(i, 1)], y_hbm.at[pl.ds(idx_ref[0, i], 1)], ssem.at[s]).start()

    def scatter_wait(s):
        for i in range(rows):
            pltpu.make_async_copy(ybuf.at[s, pl.ds(i, 1)], y_hbm.at[pl.ds(0, 1)], ssem.at[s]).wait()

    @pl.when(blk == 0)
    def _():
        ybuf[...] = jnp.zeros(ybuf.shape, ybuf.dtype)
        gather(tok_ref, 0)

    @pl.when(jnp.logical_and(blk >= 1, blk < nused))
    def _():
        scatter_wait(slot)

    @pl.when(blk < nused)
    def _():
        gather_wait(slot)
        gather(tok_next_ref, 1 - slot)
        scatter(dst_prev_ref, 1 - slot)
        p = xbuf[slot]
        xa = lax.bitcast_convert_type(p << 16, F32)
        xb = lax.bitcast_convert_type(p & jnp.uint32(0xFFFF0000), F32)
        x = jnp.concatenate([xa, xb], axis=1).astype(BF16)
        gu = _dot(x, wgu_ref[...]) + bgu_ref[...]
        x_glu = jnp.minimum(gu[:, :D_FF], SWIGLU_LIMIT)
        x_lin = jnp.clip(gu[:, D_FF:], -SWIGLU_LIMIT, SWIGLU_LIMIT)
        act = (x_lin + 1.0) * (x_glu * jax.nn.sigmoid(SWIGLU_ALPHA * x_glu))
        ybuf[slot] = _dot(act.astype(BF16), wdn_ref[...]) + bdn_ref[...]

    @pl.when(blk == nused - 1)
    def _():
        gather_wait(1 - slot)
        scatter_wait(1 - slot)
        scatter(dst_ref, slot)
        scatter_wait(slot)


def _experts(block_e, nused, tok_win, dst_win, h2p, w_gu, b_gu, w_dn, b_dn, n_rows_out):
    n_blocks = tok_win.shape[0]
    wd = h2p.shape[1]
    smem_win = lambda f: pl.BlockSpec((None, 1, EXPERT_BLOCK), f, memory_space=pltpu.SMEM)
    last = n_blocks - 1
    grid_spec = pltpu.PrefetchScalarGridSpec(
        num_scalar_prefetch=2,
        grid=(n_blocks,),
        in_specs=[smem_win(lambda i, be, nu: (i, 0, 0)),
                  smem_win(lambda i, be, nu: (jnp.minimum(jnp.minimum(i + 1, nu[0] - 1), last), 0, 0)),
                  smem_win(lambda i, be, nu: (i, 0, 0)),
                  smem_win(lambda i, be, nu: (i, 0, 0)),
                  pl.BlockSpec(memory_space=pl.ANY),
                  pl.BlockSpec((None, D_MODEL, 2 * D_FF), lambda i, be, nu: (be[i], 0, 0)),
                  pl.BlockSpec((None, 1, 2 * D_FF), lambda i, be, nu: (be[i], 0, 0)),
                  pl.BlockSpec((None, D_FF, D_MODEL), lambda i, be, nu: (be[i], 0, 0)),
                  pl.BlockSpec((None, 1, D_MODEL), lambda i, be, nu: (be[i], 0, 0))],
        out_specs=pl.BlockSpec(memory_space=pl.ANY),
        scratch_shapes=[pltpu.VMEM((2, EXPERT_BLOCK, wd), jnp.uint32),
                        pltpu.VMEM((2, EXPERT_BLOCK, D_MODEL), F32),
                        pltpu.SemaphoreType.DMA((2,)),
                        pltpu.SemaphoreType.DMA((2,))],
    )
    dump = n_rows_out - EXPERT_BLOCK + jnp.arange(EXPERT_BLOCK, dtype=jnp.int32)
    dst_prev = jnp.concatenate([dump[None], dst_win[:-1]], axis=0)
    win3 = lambda a: a.reshape(n_blocks, 1, EXPERT_BLOCK)
    return pl.pallas_call(
        _expert_kernel,
        grid_spec=grid_spec,
        out_shape=jax.ShapeDtypeStruct((n_rows_out, D_MODEL), F32),
        compiler_params=_cparams("arbitrary"),
        name="experts",
    )(block_e, nused, win3(tok_win), win3(tok_win), win3(dst_prev), win3(dst_win), h2p, w_gu,
      b_gu.reshape(N_EXPERTS, 1, 2 * D_FF), w_dn, b_dn.reshape(N_EXPERTS, 1, D_MODEL))


def _combine_kernel(*refs):
    y_refs = refs[:TOP_K]
    x1_ref, w_ref, mod_ref, g_ref, o_ref = refs[TOP_K:]
    w = w_ref[...]
    f = y_refs[0][...] * w[:, TOP_K:TOP_K + 1]
    for k in range(1, TOP_K):
        f = f + y_refs[k][...] * w[:, TOP_K + k:TOP_K + k + 1]
    o_ref[...] = x1_ref[...] + mod_ref[5:6, :] * (_rms(f) * g_ref[...])


def _combine(yb, x1, route, mod, g_post, *, tm):
    b, s, d = x1.shape
    nt = s // tm
    tok = lambda i, j: (i, j, 0)
    plane = lambda k: pl.BlockSpec((tm, d), lambda i, j: (k * b * nt + i * nt + j, 0))
    return pl.pallas_call(
        _combine_kernel,
        grid=(b, nt),
        in_specs=[plane(k) for k in range(TOP_K)] + [
            pl.BlockSpec((None, tm, d), tok),
            pl.BlockSpec((None, tm, LANES), tok),
            pl.BlockSpec((None, 6, d), lambda i, j: (i, 0, 0)),
            pl.BlockSpec((1, d), lambda i, j: (0, 0))],
        out_specs=pl.BlockSpec((None, tm, d), tok),
        out_shape=jax.ShapeDtypeStruct((b, s, d), F32),
        compiler_params=_cparams("arbitrary", "arbitrary"),
        name="combine",
    )(*([yb] * TOP_K), x1, route, mod, g_post)


def _rope_tables(seq):
    half = HEAD_DIM // 2
    nf = half // 2
    freqs = ROPE_THETA ** (-jnp.arange(nf, dtype=F32) / nf)
    t = jnp.arange(seq, dtype=jnp.int32)
    row = (t // GRID_W).astype(F32)[:, None] * freqs
    col = (t % GRID_W).astype(F32)[:, None] * freqs
    cos = jnp.concatenate([jnp.cos(row), jnp.cos(row), jnp.cos(col), jnp.cos(col)], axis=1)
    sin = jnp.concatenate([-jnp.sin(row), jnp.sin(row), -jnp.sin(col), jnp.sin(col)], axis=1)
    reps = LANES // HEAD_DIM
    return jnp.tile(cos, (1, reps)), jnp.tile(sin, (1, reps))


def _routing(top_e, n_tok):
    n_assign = n_tok * TOP_K
    flat_e = top_e.reshape(-1)
    order = jnp.argsort(flat_e).astype(jnp.int32)
    counts = jnp.sum((flat_e[:, None] == jnp.arange(N_EXPERTS, dtype=jnp.int32)[None, :]).astype(jnp.int32), axis=0)
    end = jnp.cumsum(counts)
    off = end - counts
    nblk = (counts + EXPERT_BLOCK - 1) // EXPERT_BLOCK
    blk_end = jnp.cumsum(nblk)
    blk_start = blk_end - nblk
    nused = blk_end[-1]
    n_blocks = n_assign // EXPERT_BLOCK + N_EXPERTS
    blk = jnp.minimum(jnp.arange(n_blocks, dtype=jnp.int32), nused - 1)
    block_e = jnp.sum((blk_end[None, :] <= blk[:, None]).astype(jnp.int32), axis=1)
    base = off[block_e] + (blk - blk_start[block_e]) * EXPERT_BLOCK
    lane = jnp.arange(EXPERT_BLOCK, dtype=jnp.int32)
    valid = (base[:, None] + lane[None, :]) < end[block_e][:, None]
    order_pad = jnp.concatenate([order, jnp.zeros((EXPERT_BLOCK,), jnp.int32)])
    win = jax.vmap(lambda s: lax.dynamic_slice(order_pad, (s,), (EXPERT_BLOCK,)))(base)
    tok_win = win // TOP_K
    dst_win = jnp.where(valid, (win % TOP_K) * n_tok + tok_win, TOP_K * n_tok + lane[None, :])
    return block_e.astype(jnp.int32), nused.astype(jnp.int32).reshape(1), tok_win, dst_win


def kernel(x, c, ctx, c_ctx, w_mod, b_mod, g_pre_mix, g_post_mix, g_pre_ffn, g_post_ffn, w_in, rpb, g_qnorm,
           g_knorm, w_out_a, w_out_b, w_o, w_router, b_router, w_gu, b_gu, w_dn, b_dn):
    b, s, d = x.shape
    n_ctx = ctx.shape[1]
    assert w_mod.shape[0] == 1, "single layer"
    n_tok = b * s

    cin = jnp.concatenate([c, c_ctx[None, :], jnp.zeros((8 - b - 1, d), F32)], axis=0)
    mod = _modulation(cin, w_mod[0], b_mod[0])
    mod_x = mod[:b].reshape(b, 6, d)
    mod_c = jnp.broadcast_to(mod[b].reshape(1, 6, d), (b, 6, d))

    w_in_bf = w_in[0].astype(BF16)
    cos, sin = _rope_tables(s)
    gq = jnp.tile(g_qnorm[0], GQA_HEADS).reshape(1, GQA_WIDTH)
    gk = jnp.tile(g_knorm[0], GQA_KV_HEADS).reshape(1, GQA_KV_WIDTH)
    head_id = np.arange(GQA_WIDTH) // HEAD_DIM
    ones_bd = jnp.asarray(head_id[:, None] == head_id[None, :], BF16)
    g_pre = g_pre_mix[0].reshape(1, d)

    k_a, v_a, k_b, v_b, q_a, q_b = _inproj(x, mod_x, g_pre, w_in_bf[:, :QKV_COLS], cos, sin, gq, gk, ones_bd,
                                           tm=512, rope=True, with_q=True)
    k_ca, v_ca, k_cb, v_cb = _inproj(ctx, mod_c, g_pre, w_in_bf[:, :KV_COLS], cos[:n_ctx], sin[:n_ctx],
                                           gq, gk, ones_bd, tm=n_ctx, rope=False, with_q=False)

    o_a = _natten(q_a, k_a, v_a, k_ca, v_ca, _na_bias_tables(rpb[0]))

    sk = s + n_ctx
    k_all = jnp.concatenate([k_b, k_cb], axis=1)
    v_all = jnp.concatenate([v_b, v_cb], axis=1)
    nch = sk // GQA_KC
    kt = k_all.reshape(b, nch, GQA_KC, GQA_KV_HEADS, HEAD_DIM).transpose(0, 3, 1, 4, 2)
    v_heads = v_all.reshape(b, nch, GQA_KC, GQA_KV_HEADS, HEAD_DIM).transpose(0, 3, 1, 2, 4)
    v_ext = jnp.concatenate([v_heads, jnp.ones_like(v_heads)], axis=-1)
    o_b = _gqa(q_b, kt, v_ext)

    wr = jnp.zeros((d, LANES), F32).at[:, :N_EXPERTS].set(w_router[0])
    wrh, wrl = _split_bf16(wr)
    br = jnp.full((1, LANES), NEG_BIG, F32).at[0, :N_EXPERTS].set(b_router[0])
    x1, h2p, route = _merge(x, o_a, o_b, mod_x, g_pre, g_post_mix[0].reshape(1, d), g_pre_ffn[0].reshape(1, d),
                            w_in_bf[:, QKV_COLS:], w_out_a[0].astype(BF16), w_out_b[0].astype(BF16),
                            w_o[0].astype(BF16), wrh, wrl, br, tm=256)

    top_e = route[..., :TOP_K].astype(jnp.int32).reshape(n_tok, TOP_K)
    block_e, nused, tok_win, dst_win = _routing(top_e, n_tok)
    yb = _experts(block_e, nused, tok_win, dst_win, h2p.reshape(n_tok, d // 2), w_gu[0].astype(BF16), b_gu[0],
                  w_dn[0].astype(BF16), b_dn[0], TOP_K * n_tok + EXPERT_BLOCK)
    return _combine(yb, x1, route, mod_x, g_post_ffn[0].reshape(1, d), tm=256)
```

```python
import functools

import numpy as np
import jax
import jax.numpy as jnp
from jax import lax
from jax.experimental import pallas as pl
from jax.experimental.pallas import tpu as pltpu

F32 = jnp.float32
BF16 = jnp.bfloat16

D_MODEL = 1024
GRID_W = 64
HEAD_DIM = 64
NA_HEADS = 8
NA_WIN_R = 8
NA_WIN_C = 16
GQA_HEADS = 8
GQA_KV_HEADS = 2
ROPE_THETA = 10000.0
N_EXPERTS = 32
TOP_K = 4
D_FF = D_MODEL
SWIGLU_LIMIT = 7.0
SWIGLU_ALPHA = 1.702
EXPERT_BLOCK = 128
NORM_EPS = 1e-6

NA_WIDTH = NA_HEADS * HEAD_DIM
GQA_WIDTH = GQA_HEADS * HEAD_DIM
GQA_KV_WIDTH = GQA_KV_HEADS * HEAD_DIM
KV_COLS = 2 * NA_WIDTH + 2 * GQA_KV_WIDTH
QKV_COLS = KV_COLS + NA_WIDTH + GQA_WIDTH
GATE_COLS = 2 * D_MODEL

LANES = 128
NEG_BIG = -1e30
VMEM_LIMIT = 56 * 1024 * 1024

NA_Q_ROWS = 2
NA_K_ROWS = 10
GQA_TQ = 128
GQA_KC = 256
GQA_CHUNKS_PER_STEP = 2


def _cparams(*sem):
    return pltpu.CompilerParams(dimension_semantics=sem, vmem_limit_bytes=VMEM_LIMIT)


def _split_bf16(a):
    hi = a.astype(BF16)
    lo = (a - hi.astype(F32)).astype(BF16)
    return hi, lo


def _dot(a, b):
    return jnp.dot(a, b, preferred_element_type=F32)


def _dot_nt(a, b):
    return lax.dot_general(a, b, (((1,), (1,)), ((), ())), preferred_element_type=F32)


def _rms(xf):
    return xf * lax.rsqrt(jnp.mean(xf * xf, axis=-1, keepdims=True) + NORM_EPS)


def _mod_kernel(c_ref, whi_ref, wlo_ref, b_ref, o_ref):
    c = c_ref[...]
    s = c * jax.nn.sigmoid(c)
    shi, slo = _split_bf16(s)
    whi = whi_ref[...]
    o_ref[...] = _dot(shi, whi) + _dot(slo, whi) + _dot(shi, wlo_ref[...]) + b_ref[...]


def _modulation(cin, w_mod, b_mod):
    n = w_mod.shape[1]
    bn = 1024
    whi, wlo = _split_bf16(w_mod)
    return pl.pallas_call(
        _mod_kernel,
        grid=(n // bn,),
        in_specs=[pl.BlockSpec((8, D_MODEL), lambda j: (0, 0)),
                  pl.BlockSpec((D_MODEL, bn), lambda j: (0, j)),
                  pl.BlockSpec((D_MODEL, bn), lambda j: (0, j)),
                  pl.BlockSpec((1, bn), lambda j: (0, j))],
        out_specs=pl.BlockSpec((8, bn), lambda j: (0, j)),
        out_shape=jax.ShapeDtypeStruct((8, n), F32),
        compiler_params=_cparams("arbitrary"),
        name="mod",
    )(cin, whi, wlo, b_mod.reshape(1, n))


def _head_norm(y, ones_bd, g):
    hi, lo = _split_bf16(y * y)
    ms = (_dot(hi, ones_bd) + _dot(lo, ones_bd)) * (1.0 / HEAD_DIM)
    return y * lax.rsqrt(ms + NORM_EPS) * g


def _rope(y, cos, sin):
    w = y.shape[1]
    lane = lax.broadcasted_iota(jnp.int32, y.shape, 1)
    up = pltpu.roll(y, w - 16, 1)
    dn = pltpu.roll(y, 16, 1)
    sw = jnp.where((lane % 32) < 16, up, dn)
    return y * cos + sw * sin


def _inproj_kernel(x_ref, mod_ref, g_ref, w_ref, cos_ref, sin_ref, gq_ref, gk_ref, ones_ref,
                   ka_ref, va_ref, kb_ref, vb_ref, *q_refs, rope, with_q):
    scale = HEAD_DIM ** -0.5
    if with_q:
        qa_ref, qb_ref = q_refs
    xf = x_ref[...]
    h = _rms(xf) * g_ref[...] * (1.0 + mod_ref[1:2, :]) + mod_ref[0:1, :]
    y = _dot(h.astype(BF16), w_ref[...])
    ka_ref[...] = y[:, 0:NA_WIDTH].astype(BF16)
    va_ref[...] = y[:, NA_WIDTH:2 * NA_WIDTH].astype(BF16)
    o = 2 * NA_WIDTH
    kb = _head_norm(y[:, o:o + GQA_KV_WIDTH], ones_ref[0:GQA_KV_WIDTH, 0:GQA_KV_WIDTH], gk_ref[...])
    if rope:
        cos = cos_ref[...]
        sin = sin_ref[...]
        kb = _rope(kb, cos, sin)
    kb_ref[...] = kb.astype(BF16)
    vb_ref[...] = y[:, o + GQA_KV_WIDTH:KV_COLS].astype(BF16)
    if with_q:
        qa_ref[...] = (y[:, KV_COLS:KV_COLS + NA_WIDTH] * scale).astype(BF16)
        qb = _head_norm(y[:, KV_COLS + NA_WIDTH:QKV_COLS], ones_ref[...], gq_ref[...])
        if rope:
            reps = GQA_WIDTH // LANES
            qb = _rope(qb, jnp.concatenate([cos] * reps, axis=1), jnp.concatenate([sin] * reps, axis=1))
        qb_ref[...] = (qb * scale).astype(BF16)


def _inproj(x, mod, g_pre, w, cos, sin, gq, gk, ones_bd, *, tm, rope, with_q):
    b, s, d = x.shape
    ncols = w.shape[1]
    tok = lambda i, j: (i, j, 0)
    const = lambda i, j: (0, 0)
    out_w = [NA_WIDTH, NA_WIDTH, GQA_KV_WIDTH, GQA_KV_WIDTH] + ([NA_WIDTH, GQA_WIDTH] if with_q else [])
    kern = functools.partial(_inproj_kernel, rope=rope, with_q=with_q)
    return pl.pallas_call(
        kern,
        grid=(b, s // tm),
        in_specs=[pl.BlockSpec((None, tm, d), tok),
                  pl.BlockSpec((None, 6, d), lambda i, j: (i, 0, 0)),
                  pl.BlockSpec((1, d), const),
                  pl.BlockSpec((d, ncols), const),
                  pl.BlockSpec((tm, LANES), lambda i, j: (j, 0)),
                  pl.BlockSpec((tm, LANES), lambda i, j: (j, 0)),
                  pl.BlockSpec((1, GQA_WIDTH), const),
                  pl.BlockSpec((1, GQA_KV_WIDTH), const),
                  pl.BlockSpec((GQA_WIDTH, GQA_WIDTH), const)],
        out_specs=[pl.BlockSpec((None, tm, wd), tok) for wd in out_w],
        out_shape=[jax.ShapeDtypeStruct((b, s, wd), BF16) for wd in out_w],
        compiler_params=_cparams("arbitrary", "arbitrary"),
        name="inproj_rope" if rope else "inproj_ctx",
    )(x, mod, g_pre, w, cos, sin, gq, gk, ones_bd)


def _na_base(i):
    return jnp.clip(i * NA_Q_ROWS - NA_WIN_R // 2, 0, GRID_W - NA_K_ROWS)


def _na_block_offsets():
    qr = np.arange(GRID_W // NA_Q_ROWS) * NA_Q_ROWS
    base = np.clip(qr - NA_WIN_R // 2, 0, GRID_W - NA_K_ROWS)
    reps = {}
    for q, o in zip(qr, qr - base):
        reps.setdefault(int(o), int(q))
    offs = sorted(reps)
    assert offs == list(range(0, offs[-1] + 1, NA_Q_ROWS)), offs
    return offs, reps


def _na_bias_tables(rpb):
    c = np.arange(GRID_W)
    c0 = np.clip(c - NA_WIN_C // 2, 0, GRID_W - NA_WIN_C)
    col_ok = (c[None, :] >= c0[:, None]) & (c[None, :] < c0[:, None] + NA_WIN_C)
    col_off = c[None, :] - c[:, None] + NA_WIN_C - 1
    onehot = (col_off[None] == np.arange(2 * NA_WIN_C - 1)[:, None, None]) & col_ok[None]
    e = jnp.einsum('hro,ock->hrck', rpb, jnp.asarray(onehot, F32), precision=lax.Precision.HIGHEST)
    e = e + jnp.asarray(np.where(col_ok, 0.0, NEG_BIG), F32)
    neg = jnp.full((NA_HEADS, GRID_W, GRID_W), NEG_BIG, F32)
    offs, reps = _na_block_offsets()
    tabs = []
    for o in offs:
        qr = reps[o]
        base = qr - o
        blk_rows = []
        for rr in range(NA_Q_ROWS):
            r = qr + rr
            r0 = min(max(r - NA_WIN_R // 2, 0), GRID_W - NA_WIN_R)
            pieces = [e[:, kr - r + NA_WIN_R - 1] if r0 <= kr < r0 + NA_WIN_R else neg
                      for kr in range(base, base + NA_K_ROWS)]
            blk_rows.append(jnp.concatenate(pieces, axis=-1))
        tabs.append(jnp.concatenate(blk_rows, axis=1))
    return jnp.stack(tabs)


def _natten_kernel(q_ref, k_ref, v_ref, kc_ref, vc_ref, bias_ref, o_ref):
    i = pl.program_id(1)
    start = pl.multiple_of(_na_base(i) * GRID_W, GRID_W)
    nk = NA_K_ROWS * GRID_W
    slab = 4 * HEAD_DIM
    head_of_lane = lax.broadcasted_iota(jnp.int32, (1, slab), 1) // HEAD_DIM
    for sl in range(NA_HEADS // 4):
        cols = slice(sl * slab, (sl + 1) * slab)
        q4 = q_ref[:, cols]
        k4 = k_ref[pl.ds(start, nk), cols]
        v4 = v_ref[pl.ds(start, nk), cols]
        kc4 = kc_ref[:, cols]
        vc4 = vc_ref[:, cols]
        acc = jnp.zeros((q4.shape[0], slab), F32)
        for g in range(4):
            sel = head_of_lane == g
            qm = jnp.where(sel, q4, jnp.zeros_like(q4))
            s_loc = _dot_nt(qm, k4) + bias_ref[sl * 4 + g]
            s_ctx = _dot_nt(qm, kc4)
            m = jnp.maximum(jnp.max(s_loc, axis=-1, keepdims=True), jnp.max(s_ctx, axis=-1, keepdims=True))
            p_loc = jnp.exp(s_loc - m)
            p_ctx = jnp.exp(s_ctx - m)
            l = jnp.sum(p_loc, axis=-1, keepdims=True) + jnp.sum(p_ctx, axis=-1, keepdims=True)
            o = _dot(p_loc.astype(BF16), v4) + _dot(p_ctx.astype(BF16), vc4)
            acc = acc + jnp.where(sel, o * (1.0 / l), 0.0)
        o_ref[:, cols] = acc.astype(BF16)


def _natten(q_a, k_a, v_a, k_ca, v_ca, bias):
    b, s, w = q_a.shape
    c = k_ca.shape[1]
    tq = NA_Q_ROWS * GRID_W
    nk = NA_K_ROWS * GRID_W
    full = lambda i, j: (i, 0, 0)
    return pl.pallas_call(
        _natten_kernel,
        grid=(b, s // tq),
        in_specs=[pl.BlockSpec((None, tq, w), lambda i, j: (i, j, 0)),
                  pl.BlockSpec((None, s, w), full),
                  pl.BlockSpec((None, s, w), full),
                  pl.BlockSpec((None, c, w), full),
                  pl.BlockSpec((None, c, w), full),
                  pl.BlockSpec((None, NA_HEADS, tq, nk),
                               lambda i, j: ((j * NA_Q_ROWS - _na_base(j)) // NA_Q_ROWS, 0, 0, 0))],
        out_specs=pl.BlockSpec((None, tq, w), lambda i, j: (i, j, 0)),
        out_shape=jax.ShapeDtypeStruct((b, s, w), BF16),
        compiler_params=_cparams("arbitrary", "arbitrary"),
        name="natten",
    )(q_a, k_a, v_a, k_ca, v_ca, bias)


def _gqa_kernel(q_ref, kt_ref, v_ref, o_ref):
    group = GQA_HEADS // GQA_KV_HEADS
    n_chunks = kt_ref.shape[1]
    tq = q_ref.shape[0]
    qs = [jnp.concatenate(
        [q_ref[:, (j * group + g) * HEAD_DIM:(j * group + g + 1) * HEAD_DIM] for g in range(group)], axis=0)
        for j in range(GQA_KV_HEADS)]

    def step(carry, chunks):
        out = []
        for j in range(GQA_KV_HEADS):
            s = jnp.concatenate([_dot(qs[j], kt_ref[j, c]) for c in chunks], axis=1)
            m_new = jnp.max(s, axis=-1, keepdims=True)
            if carry is not None:
                m, acc = carry[j]
                m_new = jnp.maximum(m, m_new)
            p = jnp.exp(s - m_new).astype(BF16)
            pv = _dot(p[:, :GQA_KC], v_ref[j, chunks[0]])
            for n, c in enumerate(chunks[1:], start=1):
                pv = pv + _dot(p[:, n * GQA_KC:(n + 1) * GQA_KC], v_ref[j, c])
            if carry is not None:
                pv = jnp.exp(m - m_new) * acc + pv
            out.append((m_new, pv))
        return tuple(out)

    per = GQA_CHUNKS_PER_STEP
    n_first = per + n_chunks % per
    carry = step(None, list(range(n_first)))
    carry = lax.fori_loop(0, (n_chunks - n_first) // per,
                          lambda i, cr: step(cr, [n_first + i * per + n for n in range(per)]), carry)
    for j in range(GQA_KV_HEADS):
        acc = carry[j][1]
        o = acc[:, :HEAD_DIM] * (1.0 / acc[:, HEAD_DIM:HEAD_DIM + 1])
        for g in range(group):
            h = j * group + g
            o_ref[:, h * HEAD_DIM:(h + 1) * HEAD_DIM] = o[g * tq:(g + 1) * tq].astype(BF16)


def _gqa(q_b, kt, v):
    b, s, w = q_b.shape
    _, nkv, nch, hd, kc = kt.shape
    return pl.pallas_call(
        _gqa_kernel,
        grid=(b, s // GQA_TQ),
        in_specs=[pl.BlockSpec((None, GQA_TQ, w), lambda i, j: (i, j, 0)),
                  pl.BlockSpec((None, nkv, nch, hd, kc), lambda i, j: (i, 0, 0, 0, 0)),
                  pl.BlockSpec((None, nkv, nch, kc, 2 * HEAD_DIM), lambda i, j: (i, 0, 0, 0, 0))],
        out_specs=pl.BlockSpec((None, GQA_TQ, w), lambda i, j: (i, j, 0)),
        out_shape=jax.ShapeDtypeStruct((b, s, w), BF16),
        compiler_params=_cparams("arbitrary", "arbitrary"),
        name="gqa",
    )(q_b, kt, v)


def _merge_kernel(x_ref, oa_ref, ob_ref, mod_ref, g1_ref, g2_ref, g3_ref, wg_ref, woa_ref, wob_ref, wo_ref,
                  wrh_ref, wrl_ref, br_ref, x1_ref, h2_ref, route_ref):
    xf = x_ref[...]
    h = _rms(xf) * g1_ref[...] * (1.0 + mod_ref[1:2, :]) + mod_ref[0:1, :]
    gates = _dot(h.astype(BF16), wg_ref[...])
    ya = _dot(oa_ref[...], woa_ref[...])
    yb = _dot(ob_ref[...], wob_ref[...])
    z = jax.nn.sigmoid(gates[:, :D_MODEL]) * ya + jax.nn.sigmoid(gates[:, D_MODEL:]) * yb
    y = _dot(z.astype(BF16), wo_ref[...])
    x1 = xf + mod_ref[2:3, :] * (_rms(y) * g2_ref[...])
    x1_ref[...] = x1
    h2 = _rms(x1) * g3_ref[...] * (1.0 + mod_ref[4:5, :]) + mod_ref[3:4, :]

    half = D_MODEL // 2
    lo_bits = lax.bitcast_convert_type(h2[:, :half].astype(BF16).astype(F32), jnp.uint32) >> 16
    hi_bits = lax.bitcast_convert_type(h2[:, half:].astype(BF16).astype(F32), jnp.uint32) & jnp.uint32(0xFFFF0000)
    h2_ref[...] = lo_bits | hi_bits

    hh, hl = _split_bf16(h2)
    wrh = wrh_ref[...]
    lg = _dot(hh, wrh) + _dot(hl, wrh) + _dot(hh, wrl_ref[...]) + br_ref[...]
    lane = lax.broadcasted_iota(jnp.int32, lg.shape, 1)
    vals, idxs = [], []
    for _ in range(TOP_K):
        m = jnp.max(lg, axis=-1, keepdims=True)
        idx = jnp.min(jnp.where(lg == m, lane, LANES), axis=-1, keepdims=True)
        vals.append(m)
        idxs.append(idx)
        lg = jnp.where(lane == idx, -3e38, lg)
    es = [jnp.exp(v - vals[0]) for v in vals]
    inv = 1.0 / (es[0] + es[1] + es[2] + es[3])
    route = jnp.zeros(lg.shape, F32)
    for k in range(TOP_K):
        route = jnp.where(lane == k, idxs[k].astype(F32), route)
        route = jnp.where(lane == TOP_K + k, es[k] * inv, route)
    route_ref[...] = route


def _merge(x, o_a, o_b, mod, g1, g2, g3, wg, woa, wob, wo, wrh, wrl, br, *, tm):
    b, s, d = x.shape
    tok = lambda i, j: (i, j, 0)
    const = lambda i, j: (0, 0)
    return pl.pallas_call(
        _merge_kernel,
        grid=(b, s // tm),
        in_specs=[pl.BlockSpec((None, tm, d), tok),
                  pl.BlockSpec((None, tm, NA_WIDTH), tok),
                  pl.BlockSpec((None, tm, GQA_WIDTH), tok),
                  pl.BlockSpec((None, 6, d), lambda i, j: (i, 0, 0)),
                  pl.BlockSpec((1, d), const), pl.BlockSpec((1, d), const), pl.BlockSpec((1, d), const),
                  pl.BlockSpec((d, GATE_COLS), const),
                  pl.BlockSpec((NA_WIDTH, d), const),
                  pl.BlockSpec((GQA_WIDTH, d), const),
                  pl.BlockSpec((d, d), const),
                  pl.BlockSpec((d, LANES), const), pl.BlockSpec((d, LANES), const),
                  pl.BlockSpec((1, LANES), const)],
        out_specs=[pl.BlockSpec((None, tm, d), tok),
                   pl.BlockSpec((None, tm, d // 2), tok),
                   pl.BlockSpec((None, tm, LANES), tok)],
        out_shape=[jax.ShapeDtypeStruct((b, s, d), F32),
                   jax.ShapeDtypeStruct((b, s, d // 2), jnp.uint32),
                   jax.ShapeDtypeStruct((b, s, LANES), F32)],
        compiler_params=_cparams("arbitrary", "arbitrary"),
        name="merge",
    )(x, o_a, o_b, mod, g1, g2, g3, wg, woa, wob, wo, wrh, wrl, br)


def _expert_kernel(ve_ref, vb_ref, lo_ref, hi_ref, nused_ref, tok_ref, tok_next_ref, dst_prev_ref, dst_ref, h2_hbm,
                   wgu_ref, bgu_ref, wdn_ref, bdn_ref, y_hbm, xbuf, ybuf, gsem, ssem):
    blk = pl.program_id(0)
    nused = nused_ref[0]
    slot = blk % 2
    rows = EXPERT_BLOCK

    def gather(idx_ref, s):
        for i in range(rows):
            pltpu.make_async_copy(h2_hbm.at[pl.ds(idx_ref[0, i], 1)], xbuf.at[s, pl.ds(i, 1)], gsem.at[s]).start()

    def gather_wait(s):
        for i in range(rows):
            pltpu.make_async_copy(h2_hbm.at[pl.ds(0, 1)], xbuf.at[s, pl.ds(i, 1)], gsem.at[s]).wait()

    def scatter(idx_ref, s):
        for i in range(rows):
            pltpu.make_async_copy(ybuf.at[s, pl.ds(i, 1)], y_hbm.at[pl.ds(idx_ref[0, i], 1)],
                                  ssem.at[s]).start(priority=1)

    def scatter_wait(s):
        for i in range(rows):
            pltpu.make_async_copy(ybuf.at[s, pl.ds(i, 1)], y_hbm.at[pl.ds(0, 1)], ssem.at[s]).wait()

    @pl.when(blk == 0)
    def _():
        ybuf[...] = jnp.zeros(ybuf.shape, ybuf.dtype)
        gather(tok_ref, 0)

    @pl.when(jnp.logical_and(blk >= 1, blk < nused))
    def _():
        scatter_wait(slot)

    @pl.when(blk < nused)
    def _():
        gather_wait(slot)
        gather(tok_next_ref, 1 - slot)
        scatter(dst_prev_ref, 1 - slot)
        p = xbuf[slot]
        xa = lax.bitcast_convert_type(p << 16, F32)
        xb = lax.bitcast_convert_type(p & jnp.uint32(0xFFFF0000), F32)
        x = jnp.concatenate([xa, xb], axis=1).astype(BF16)
        gu = _dot(x, wgu_ref[...]) + bgu_ref[...]
        x_glu = jnp.minimum(gu[:, :D_FF], SWIGLU_LIMIT)
        x_lin = jnp.clip(gu[:, D_FF:], -SWIGLU_LIMIT, SWIGLU_LIMIT)
        act = (x_lin + 1.0) * (x_glu * jax.nn.sigmoid(SWIGLU_ALPHA * x_glu))
        y = _dot(act.astype(BF16), wdn_ref[...]) + bdn_ref[...]
        row = lax.broadcasted_iota(jnp.int32, (rows, 1), 0)
        mine = jnp.logical_and(row >= lo_ref[blk], row < hi_ref[blk])
        ybuf[slot] = jnp.where(mine, y, ybuf[1 - slot])

    @pl.when(blk == nused - 1)
    def _():
        gather_wait(1 - slot)
        scatter_wait(1 - slot)
        scatter(dst_ref, slot)
        scatter_wait(slot)


def _experts(visits, tok_win, dst_win, h2p, w_gu, b_gu, w_dn, b_dn):
    vis_e, vis_b, lo, hi, nused = visits
    n_vis = vis_e.shape[0]
    n_blocks = tok_win.shape[0]
    wd = h2p.shape[1]
    smem_win = lambda f: pl.BlockSpec((None, 1, EXPERT_BLOCK), f, memory_space=pltpu.SMEM)
    cur = lambda i, ve, vb, lo, hi, nu: (vb[i], 0, 0)
    nxt = lambda i, ve, vb, lo, hi, nu: (vb[jnp.minimum(i + 1, n_vis - 1)], 0, 0)
    prv = lambda i, ve, vb, lo, hi, nu: (vb[jnp.maximum(i - 1, 0)], 0, 0)
    wsel = lambda i, ve, vb, lo, hi, nu: (ve[i], 0, 0)
    grid_spec = pltpu.PrefetchScalarGridSpec(
        num_scalar_prefetch=5,
        grid=(n_vis,),
        in_specs=[smem_win(cur), smem_win(nxt), smem_win(prv), smem_win(cur),
                  pl.BlockSpec(memory_space=pl.ANY),
                  pl.BlockSpec((None, D_MODEL, 2 * D_FF), wsel),
                  pl.BlockSpec((None, 1, 2 * D_FF), wsel),
                  pl.BlockSpec((None, D_FF, D_MODEL), wsel),
                  pl.BlockSpec((None, 1, D_MODEL), wsel)],
        out_specs=pl.BlockSpec(memory_space=pl.ANY),
        scratch_shapes=[pltpu.VMEM((2, EXPERT_BLOCK, wd), jnp.uint32),
                        pltpu.VMEM((2, EXPERT_BLOCK, D_MODEL), F32),
                        pltpu.SemaphoreType.DMA((2,)),
                        pltpu.SemaphoreType.DMA((2,))],
    )
    win3 = lambda a: a.reshape(n_blocks, 1, EXPERT_BLOCK)
    return pl.pallas_call(
        _expert_kernel,
        grid_spec=grid_spec,
        out_shape=jax.ShapeDtypeStruct((n_blocks * EXPERT_BLOCK, D_MODEL), F32),
        compiler_params=_cparams("arbitrary"),
        name="experts",
    )(vis_e, vis_b, lo, hi, nused, win3(tok_win), win3(tok_win), win3(dst_win), win3(dst_win), h2p, w_gu,
      b_gu.reshape(N_EXPERTS, 1, 2 * D_FF), w_dn, b_dn.reshape(N_EXPERTS, 1, D_MODEL))


def _combine_kernel(*refs):
    y_refs = refs[:TOP_K]
    x1_ref, w_ref, mod_ref, g_ref, o_ref = refs[TOP_K:]
    w = w_ref[...]
    f = y_refs[0][...] * w[:, TOP_K:TOP_K + 1]
    for k in range(1, TOP_K):
        f = f + y_refs[k][...] * w[:, TOP_K + k:TOP_K + k + 1]
    o_ref[...] = x1_ref[...] + mod_ref[5:6, :] * (_rms(f) * g_ref[...])


def _combine(yb, x1, route, mod, g_post, *, tm):
    b, s, d = x1.shape
    nt = s // tm
    tok = lambda i, j: (i, j, 0)
    plane = lambda k: pl.BlockSpec((tm, d), lambda i, j: (k * b * nt + i * nt + j, 0))
    return pl.pallas_call(
        _combine_kernel,
        grid=(b, nt),
        in_specs=[plane(k) for k in range(TOP_K)] + [
            pl.BlockSpec((None, tm, d), tok),
            pl.BlockSpec((None, tm, LANES), tok),
            pl.BlockSpec((None, 6, d), lambda i, j: (i, 0, 0)),
            pl.BlockSpec((1, d), lambda i, j: (0, 0))],
        out_specs=pl.BlockSpec((None, tm, d), tok),
        out_shape=jax.ShapeDtypeStruct((b, s, d), F32),
        compiler_params=_cparams("arbitrary", "arbitrary"),
        name="combine",
    )(*([yb] * TOP_K), x1, route, mod, g_post)


def _rope_tables(seq):
    half = HEAD_DIM // 2
    nf = half // 2
    freqs = ROPE_THETA ** (-jnp.arange(nf, dtype=F32) / nf)
    t = jnp.arange(seq, dtype=jnp.int32)
    row = (t // GRID_W).astype(F32)[:, None] * freqs
    col = (t % GRID_W).astype(F32)[:, None] * freqs
    cos = jnp.concatenate([jnp.cos(row), jnp.cos(row), jnp.cos(col), jnp.cos(col)], axis=1)
    sin = jnp.concatenate([-jnp.sin(row), jnp.sin(row), -jnp.sin(col), jnp.sin(col)], axis=1)
    reps = LANES // HEAD_DIM
    return jnp.tile(cos, (1, reps)), jnp.tile(sin, (1, reps))


def _routing(top_e, n_tok):
    n_assign = n_tok * TOP_K
    assert n_assign % EXPERT_BLOCK == 0
    n_blocks = n_assign // EXPERT_BLOCK
    flat_e = top_e.reshape(-1)
    order = jnp.argsort(flat_e).astype(jnp.int32)
    experts = jnp.arange(N_EXPERTS, dtype=jnp.int32)
    counts = jnp.sum((flat_e[:, None] == experts[None, :]).astype(jnp.int32), axis=0)
    end = jnp.cumsum(counts)
    off = end - counts
    first_blk = off // EXPERT_BLOCK
    n_vis_e = jnp.where(counts > 0, (end - 1) // EXPERT_BLOCK - first_blk + 1, 0)
    vis_end = jnp.cumsum(n_vis_e)
    vis_start = vis_end - n_vis_e
    nused = vis_end[-1]
    n_vis = n_blocks + N_EXPERTS - 1
    v = jnp.minimum(jnp.arange(n_vis, dtype=jnp.int32), nused - 1)
    vis_e = jnp.sum((vis_end[None, :] <= v[:, None]).astype(jnp.int32), axis=1)
    vis_b = first_blk[vis_e] + v - vis_start[vis_e]
    lo = jnp.clip(off[vis_e] - vis_b * EXPERT_BLOCK, 0, EXPERT_BLOCK)
    hi = jnp.clip(end[vis_e] - vis_b * EXPERT_BLOCK, 0, EXPERT_BLOCK)
    i32 = lambda a: a.astype(jnp.int32)
    tok_win = (order // TOP_K).reshape(n_blocks, EXPERT_BLOCK)
    dst_win = ((order % TOP_K) * n_tok + order // TOP_K).reshape(n_blocks, EXPERT_BLOCK)
    return (i32(vis_e), i32(vis_b), i32(lo), i32(hi), i32(nused).reshape(1)), tok_win, dst_win


def kernel(x, c, ctx, c_ctx, w_mod, b_mod, g_pre_mix, g_post_mix, g_pre_ffn, g_post_ffn, w_in, rpb, g_qnorm,
           g_knorm, w_out_a, w_out_b, w_o, w_router, b_router, w_gu, b_gu, w_dn, b_dn):
    b, s, d = x.shape
    n_ctx = ctx.shape[1]
    assert w_mod.shape[0] == 1, "single layer"
    n_tok = b * s

    cin = jnp.concatenate([c, c_ctx[None, :], jnp.zeros((8 - b - 1, d), F32)], axis=0)
    mod = _modulation(cin, w_mod[0], b_mod[0])
    mod_x = mod[:b].reshape(b, 6, d)
    mod_c = jnp.broadcast_to(mod[b].reshape(1, 6, d), (b, 6, d))

    w_in_bf = w_in[0].astype(BF16)
    cos, sin = _rope_tables(s)
    gq = jnp.tile(g_qnorm[0], GQA_HEADS).reshape(1, GQA_WIDTH)
    gk = jnp.tile(g_knorm[0], GQA_KV_HEADS).reshape(1, GQA_KV_WIDTH)
    head_id = np.arange(GQA_WIDTH) // HEAD_DIM
    ones_bd = jnp.asarray(head_id[:, None] == head_id[None, :], BF16)
    g_pre = g_pre_mix[0].reshape(1, d)

    k_a, v_a, k_b, v_b, q_a, q_b = _inproj(x, mod_x, g_pre, w_in_bf[:, :QKV_COLS], cos, sin, gq, gk, ones_bd,
                                           tm=512, rope=True, with_q=True)
    k_ca, v_ca, k_cb, v_cb = _inproj(ctx, mod_c, g_pre, w_in_bf[:, :KV_COLS], cos[:n_ctx], sin[:n_ctx],
                                           gq, gk, ones_bd, tm=n_ctx, rope=False, with_q=False)

    o_a = _natten(q_a, k_a, v_a, k_ca, v_ca, _na_bias_tables(rpb[0]))

    sk = s + n_ctx
    k_all = jnp.concatenate([k_b, k_cb], axis=1)
    v_all = jnp.concatenate([v_b, v_cb], axis=1)
    nch = sk // GQA_KC
    kt = k_all.reshape(b, nch, GQA_KC, GQA_KV_HEADS, HEAD_DIM).transpose(0, 3, 1, 4, 2)
    v_heads = v_all.reshape(b, nch, GQA_KC, GQA_KV_HEADS, HEAD_DIM).transpose(0, 3, 1, 2, 4)
    v_ext = jnp.concatenate([v_heads, jnp.ones_like(v_heads)], axis=-1)
    o_b = _gqa(q_b, kt, v_ext)

    wr = jnp.zeros((d, LANES), F32).at[:, :N_EXPERTS].set(w_router[0])
    wrh, wrl = _split_bf16(wr)
    br = jnp.full((1, LANES), NEG_BIG, F32).at[0, :N_EXPERTS].set(b_router[0])
    x1, h2p, route = _merge(x, o_a, o_b, mod_x, g_pre, g_post_mix[0].reshape(1, d), g_pre_ffn[0].reshape(1, d),
                            w_in_bf[:, QKV_COLS:], w_out_a[0].astype(BF16), w_out_b[0].astype(BF16),
                            w_o[0].astype(BF16), wrh, wrl, br, tm=256)

    top_e = route[..., :TOP_K].astype(jnp.int32).reshape(n_tok, TOP_K)
    visits, tok_win, dst_win = _routing(top_e, n_tok)
    yb = _experts(visits, tok_win, dst_win, h2p.reshape(n_tok, d // 2), w_gu[0].astype(BF16), b_gu[0],
                  w_dn[0].astype(BF16), b_dn[0])
    return _combine(yb, x1, route, mod_x, g_post_ffn[0].reshape(1, d), tm=256)
```
